```python
import math
import jax, jax.numpy as jnp
from jax import lax
import numpy as np

D_MODEL = 2048
BATCH = 2
SEQ = 16384
DEPTH = 2

N_EVEN = (DEPTH + 1) // 2
N_ODD = DEPTH // 2
MEM_LEN = 256
MAX_POS_OFFSET = 4096

ALPHA = (2 * DEPTH) ** 0.25
BETA = (8 * DEPTH) ** -0.25
LN_EPS = 1e-5

ROPE_THETA = 500000.0
ROPE_FRAC = 4
Q_BLOCK = 128

DIFF_HEADS = 8
DIFF_HEAD_DIM = 64
DIFF_V_DIM = 2 * DIFF_HEAD_DIM
DIFF_WIDTH = DIFF_HEADS * DIFF_V_DIM

CONV_CH = D_MODEL - DIFF_WIDTH
CONV_WIDTH = 31

EVEN_IN = 3 * DIFF_WIDTH + 2 * CONV_CH
EVEN_SPLITS = (DIFF_WIDTH, 2 * DIFF_WIDTH, 3 * DIFF_WIDTH, 3 * DIFF_WIDTH + CONV_CH)

DSA_HEADS = 16
DSA_KV_HEADS = 4
DSA_HEAD_DIM = 128
IDX_HEADS = 4
IDX_DIM = 64
TOPK_MAX = 256
DSA_Q = DSA_HEADS * DSA_HEAD_DIM
DSA_KV = DSA_KV_HEADS * DSA_HEAD_DIM
IDX_Q = IDX_HEADS * IDX_DIM
ODD_IN = DSA_Q + 2 * DSA_KV + IDX_Q + IDX_DIM + IDX_HEADS
ODD_SPLITS = (DSA_Q, DSA_Q + DSA_KV, DSA_Q + 2 * DSA_KV,
              DSA_Q + 2 * DSA_KV + IDX_Q, DSA_Q + 2 * DSA_KV + IDX_Q + IDX_DIM)

XA_HEADS = 4
XA_HEAD_DIM = D_MODEL // XA_HEADS

D_FF = -(-(8 * D_MODEL) // (3 * 256)) * 256

kernel_name = 'hybrid_diffattn_conformer_dsa_deepnorm'


def layer_norm(x, g, b):
    xf = x.astype(jnp.float32)
    mu = jnp.mean(xf, axis=-1, keepdims=True)
    var = jnp.mean(jnp.square(xf - mu), axis=-1, keepdims=True)
    return ((xf - mu) * lax.rsqrt(var + LN_EPS) * g + b).astype(x.dtype)


def rms_norm(x, g):
    xf = x.astype(jnp.float32)
    y = xf * lax.rsqrt(jnp.mean(xf * xf, axis=-1, keepdims=True) + LN_EPS)
    return (y * g).astype(x.dtype)


def partial_rotary(x, positions):
    dh = x.shape[-1]
    rot = dh // ROPE_FRAC
    half = rot // 2
    inv_freq = 1.0 / (ROPE_THETA ** (jnp.arange(half, dtype=jnp.float32) / half))
    ang = positions.astype(jnp.float32)[..., None] * inv_freq
    cos = jnp.cos(ang)[:, :, None, :]
    sin = jnp.sin(ang)[:, :, None, :]
    xr = x[..., :rot].astype(jnp.float32)
    x1, x2 = xr[..., :half], xr[..., half:]
    rotated = jnp.concatenate([x1 * cos - x2 * sin, x2 * cos + x1 * sin], axis=-1)
    return jnp.concatenate([rotated.astype(x.dtype), x[..., rot:]], axis=-1)


def diff_attention(q, k, v, lam):
    B, S, H, _, d = q.shape
    nb = S // Q_BLOCK
    scale = d ** -0.5
    qb = q.reshape(B, nb, Q_BLOCK, H, 2, d).transpose(1, 0, 2, 3, 4, 5)
    kpos = jnp.arange(S)

    def block(args):
        q_blk, i = args
        qpos = i * Q_BLOCK + jnp.arange(Q_BLOCK)
        causal = kpos[None, :] <= qpos[:, None]
        s = jnp.einsum('bqhcd,bkhcd->bhcqk', q_blk, k).astype(jnp.float32) * scale
        p = jax.nn.softmax(jnp.where(causal, s, -jnp.inf), axis=-1)
        p_diff = p[:, :, 0] - lam * p[:, :, 1]
        return jnp.einsum('bhqk,bkhe->bqhe', p_diff.astype(v.dtype), v)

    out = lax.map(block, (qb, jnp.arange(nb)))
    return out.transpose(1, 0, 2, 3, 4).reshape(B, S, H, 2 * d)


def even_mixer(x, positions, w_in, w_out, lam_p, subln_g, conv_w, conv_b,
               conv_ln_g, conv_ln_b, lam_init):
    B, S, _ = x.shape
    q, k, v, glu_val, glu_gate = jnp.split(x @ w_in, EVEN_SPLITS, axis=-1)
    q = partial_rotary(q.reshape(B, S, 2 * DIFF_HEADS, DIFF_HEAD_DIM), positions)
    k = partial_rotary(k.reshape(B, S, 2 * DIFF_HEADS, DIFF_HEAD_DIM), positions)
    q = q.reshape(B, S, DIFF_HEADS, 2, DIFF_HEAD_DIM)
    k = k.reshape(B, S, DIFF_HEADS, 2, DIFF_HEAD_DIM)
    v = v.reshape(B, S, DIFF_HEADS, DIFF_V_DIM)
    lp = lam_p.astype(jnp.float32)
    lam = (jnp.exp(jnp.sum(lp[0] * lp[1])) - jnp.exp(jnp.sum(lp[2] * lp[3])) + lam_init)
    a = diff_attention(q, k, v, lam)
    a = (rms_norm(a, subln_g) * (1.0 - lam_init)).reshape(B, S, DIFF_WIDTH)
    u = glu_val * jax.nn.sigmoid(glu_gate)
    c = lax.conv_general_dilated(
        u, conv_w[:, None, :].astype(u.dtype), window_strides=(1,),
        padding=[(CONV_WIDTH - 1, 0)], dimension_numbers=('NWC', 'WIO', 'NWC'),
        feature_group_count=CONV_CH) + conv_b
    c = jax.nn.silu(layer_norm(c, conv_ln_g, conv_ln_b))
    return jnp.concatenate([a, c], axis=-1) @ w_out


def odd_mixer(x, positions, w_in, w_out):
    B, S, _ = x.shape
    q, k, v, qi, ki, wi = jnp.split(x @ w_in, ODD_SPLITS, axis=-1)
    q = partial_rotary(q.reshape(B, S, DSA_HEADS, DSA_HEAD_DIM), positions)
    k = partial_rotary(k.reshape(B, S, DSA_KV_HEADS, DSA_HEAD_DIM), positions)
    v = v.reshape(B, S, DSA_KV_HEADS, DSA_HEAD_DIM)
    qi = partial_rotary(qi.reshape(B, S, IDX_HEADS, IDX_DIM), positions)
    ki = partial_rotary(ki.reshape(B, S, 1, IDX_DIM), positions)[:, :, 0]
    wi = wi * (IDX_HEADS ** -0.5 * IDX_DIM ** -0.5)
    topk = min(TOPK_MAX, S // 4)
    rep = DSA_HEADS // DSA_KV_HEADS
    scale = DSA_HEAD_DIM ** -0.5
    nb = S // Q_BLOCK
    qb = q.reshape(B, nb, Q_BLOCK, DSA_KV_HEADS, rep, DSA_HEAD_DIM).transpose(1, 0, 2, 3, 4, 5)
    qib = qi.reshape(B, nb, Q_BLOCK, IDX_HEADS, IDX_DIM).transpose(1, 0, 2, 3, 4)
    wib = wi.reshape(B, nb, Q_BLOCK, IDX_HEADS).transpose(1, 0, 2, 3)
    kpos = jnp.arange(S)
    gather = jax.vmap(lambda kb, ib: kb[ib])

    def block(args):
        q_blk, qi_blk, w_blk, i = args
        qpos = i * Q_BLOCK + jnp.arange(Q_BLOCK)
        causal = kpos[None, :] <= qpos[:, None]
        rel = jax.nn.relu(jnp.einsum('bqhd,bkd->bqhk', qi_blk, ki).astype(jnp.float32))
        score = jnp.einsum('bqh,bqhk->bqk', w_blk.astype(jnp.float32), rel)
        score = jnp.where(causal[None], score, -jnp.inf)
        _, sel = lax.top_k(score, topk)
        valid = sel <= qpos[None, :, None]
        k_sel = gather(k, sel)
        v_sel = gather(v, sel)
        s = jnp.einsum('bqgrd,bqkgd->bgrqk', q_blk, k_sel).astype(jnp.float32) * scale
        p = jax.nn.softmax(jnp.where(valid[:, None, None], s, -jnp.inf), axis=-1)
        return jnp.einsum('bgrqk,bqkgd->bqgrd', p.astype(v_sel.dtype), v_sel)

    out = lax.map(block, (qb, qib, wib, jnp.arange(nb)))
    out = out.transpose(1, 0, 2, 3, 4, 5).reshape(B, S, DSA_Q)
    return out @ w_out


def memory_cross_attention(x, mem, wq, wkv, wo):
    B, S, _ = x.shape
    q = (x @ wq).reshape(B, S, XA_HEADS, XA_HEAD_DIM)
    k, v = jnp.split(mem @ wkv, 2, axis=-1)
    k = k.reshape(B, -1, XA_HEADS, XA_HEAD_DIM)
    v = v.reshape(B, -1, XA_HEADS, XA_HEAD_DIM)
    s = jnp.einsum('bqhd,bmhd->bhqm', q, k).astype(jnp.float32) * XA_HEAD_DIM ** -0.5
    p = jax.nn.softmax(s, axis=-1).astype(v.dtype)
    o = jnp.einsum('bhqm,bmhd->bqhd', p, v).reshape(B, S, D_MODEL)
    return o @ wo


def swiglu_ffn(x, w_in, w_out):
    gate, up = jnp.split(x @ w_in, 2, axis=-1)
    return (jax.nn.silu(gate) * up) @ w_out


def _normal(key, shape, scale):
    return jax.random.normal(key, shape, jnp.float32) * scale


def setup_inputs(seed: int = 0) -> dict:
    key = jax.random.key(seed)
    ks = jax.random.split(key, 24)
    D = D_MODEL
    x = _normal(ks[0], (BATCH, SEQ, D), 1.0)
    mem = _normal(ks[1], (BATCH, MEM_LEN, D), 1.0)
    positions = (jax.random.randint(ks[2], (BATCH, 1), 0, MAX_POS_OFFSET, dtype=jnp.int32)
                 + jnp.arange(SEQ, dtype=jnp.int32)[None, :])
    even_cols = jnp.concatenate([jnp.ones((2 * DIFF_WIDTH,), jnp.float32),
                                 jnp.full((DIFF_WIDTH,), BETA, jnp.float32),
                                 jnp.ones((2 * CONV_CH,), jnp.float32)])
    w_in_even = _normal(ks[3], (N_EVEN, D, EVEN_IN), D ** -0.5) * even_cols
    w_out_even = _normal(ks[4], (N_EVEN, D, D), D ** -0.5 * BETA)
    diff_lambda = _normal(ks[5], (N_EVEN, 4, DIFF_HEAD_DIM), 0.1)
    diff_subln_g = 1.0 + _normal(ks[6], (N_EVEN, DIFF_V_DIM), 0.02)
    conv_w = _normal(ks[7], (N_EVEN, CONV_WIDTH, CONV_CH), CONV_WIDTH ** -0.5)
    conv_b = _normal(ks[8], (N_EVEN, CONV_CH), 0.01)
    conv_ln_g = 1.0 + _normal(ks[9], (N_EVEN, CONV_CH), 0.02)
    conv_ln_b = _normal(ks[10], (N_EVEN, CONV_CH), 0.02)
    odd_cols = jnp.concatenate([jnp.ones((DSA_Q + DSA_KV,), jnp.float32),
                                jnp.full((DSA_KV,), BETA, jnp.float32),
                                jnp.ones((IDX_Q + IDX_DIM + IDX_HEADS,), jnp.float32)])
    w_in_odd = _normal(ks[11], (N_ODD, D, ODD_IN), D ** -0.5) * odd_cols
    w_out_odd = _normal(ks[12], (N_ODD, DSA_Q, D), DSA_Q ** -0.5 * BETA)
    xa_wq = _normal(ks[13], (DEPTH, D, D), D ** -0.5)
    xa_cols = jnp.concatenate([jnp.ones((D,), jnp.float32), jnp.full((D,), BETA, jnp.float32)])
    xa_wkv = _normal(ks[14], (DEPTH, D, 2 * D), D ** -0.5) * xa_cols
    xa_wo = _normal(ks[15], (DEPTH, D, D), D ** -0.5 * BETA)
    ffn_w_in = _normal(ks[16], (DEPTH, D, 2 * D_FF), D ** -0.5 * BETA)
    ffn_w_out = _normal(ks[17], (DEPTH, D_FF, D), D_FF ** -0.5 * BETA)
    ln_g = 1.0 + _normal(ks[18], (DEPTH, 3, D), 0.02)
    ln_b = _normal(ks[19], (DEPTH, 3, D), 0.02)
    return {'x': x, 'mem': mem, 'positions': positions,
            'w_in_even': w_in_even, 'w_out_even': w_out_even,
            'diff_lambda': diff_lambda, 'diff_subln_g': diff_subln_g,
            'conv_w': conv_w, 'conv_b': conv_b, 'conv_ln_g': conv_ln_g, 'conv_ln_b': conv_ln_b,
            'w_in_odd': w_in_odd, 'w_out_odd': w_out_odd,
            'xa_wq': xa_wq, 'xa_wkv': xa_wkv, 'xa_wo': xa_wo,
            'ffn_w_in': ffn_w_in, 'ffn_w_out': ffn_w_out,
            'ln_g': ln_g, 'ln_b': ln_b}


def reference(x, mem, positions, w_in_even, w_out_even, diff_lambda, diff_subln_g,
              conv_w, conv_b, conv_ln_g, conv_ln_b, w_in_odd, w_out_odd,
              xa_wq, xa_wkv, xa_wo, ffn_w_in, ffn_w_out, ln_g, ln_b):
    for layer in range(DEPTH):
        j = layer // 2
        if layer % 2 == 0:
            lam_init = 0.8 - 0.6 * math.exp(-0.3 * layer)
            h = even_mixer(x, positions, w_in_even[j], w_out_even[j], diff_lambda[j],
                           diff_subln_g[j], conv_w[j], conv_b[j], conv_ln_g[j],
                           conv_ln_b[j], lam_init)
        else:
            h = odd_mixer(x, positions, w_in_odd[j], w_out_odd[j])
        x = layer_norm(ALPHA * x + h, ln_g[layer, 0], ln_b[layer, 0])
        h = memory_cross_attention(x, mem, xa_wq[layer], xa_wkv[layer], xa_wo[layer])
        x = layer_norm(ALPHA * x + h, ln_g[layer, 1], ln_b[layer, 1])
        h = swiglu_ffn(x, ffn_w_in[layer], ffn_w_out[layer])
        x = layer_norm(ALPHA * x + h, ln_g[layer, 2], ln_b[layer, 2])
    return x
```

```python
import functools
import math

import jax
import jax.numpy as jnp
from jax import lax
from jax.experimental import pallas as pl
from jax.experimental.pallas import tpu as pltpu

F32 = jnp.float32
BF16 = jnp.bfloat16

D_MODEL = 2048
DEPTH = 2
ALPHA = (2 * DEPTH) ** 0.25
LN_EPS = 1e-5
ROPE_THETA = 500000.0
ROPE_FRAC = 4

DIFF_HEADS = 8
DIFF_HEAD_DIM = 64
DIFF_V_DIM = 2 * DIFF_HEAD_DIM
DIFF_WIDTH = DIFF_HEADS * DIFF_V_DIM
CONV_CH = D_MODEL - DIFF_WIDTH
CONV_WIDTH = 31
EVEN_IN = 3 * DIFF_WIDTH + 2 * CONV_CH

DSA_HEADS = 16
DSA_KV_HEADS = 4
DSA_HEAD_DIM = 128
DSA_REP = DSA_HEADS // DSA_KV_HEADS
IDX_HEADS = 4
IDX_DIM = 64
TOPK_MAX = 256
DSA_Q = DSA_HEADS * DSA_HEAD_DIM
DSA_KV = DSA_KV_HEADS * DSA_HEAD_DIM
IDX_Q = IDX_HEADS * IDX_DIM
ODD_MAIN = DSA_Q + 2 * DSA_KV
IDX_PAD = 512

XA_HEADS = 4
XA_HEAD_DIM = D_MODEL // XA_HEADS
D_FF = -(-(8 * D_MODEL) // (3 * 256)) * 256

LANES = 128
NEG = -1e30
VMEM_LIMIT = 56 * 1024 * 1024


def _cparams(sem):
    return pltpu.CompilerParams(dimension_semantics=sem, vmem_limit_bytes=VMEM_LIMIT)


def _layer_norm_rows(y, g, b):
    mu = jnp.mean(y, axis=-1, keepdims=True)
    d = y - mu
    var = jnp.mean(d * d, axis=-1, keepdims=True)
    return d * lax.rsqrt(var + LN_EPS) * g + b


def _rot_tables(positions, head_dim, active_lanes=LANES):
    rot = head_dim // ROPE_FRAC
    half = rot // 2
    inv_freq = 1.0 / (ROPE_THETA ** (jnp.arange(half, dtype=F32) / half))
    ang = positions.reshape(-1, 1).astype(F32) * inv_freq
    cos, sin = jnp.cos(ang), jnp.sin(ang)
    lane = jnp.arange(LANES)
    d = lane % head_dim
    f = d % half
    live = lane < active_lanes
    cos_l = jnp.take(cos, f, axis=1)
    sin_l = jnp.take(sin, f, axis=1)
    c = jnp.where((d < rot) & live, cos_l, 1.0)
    s1 = jnp.where((d >= half) & (d < rot) & live, sin_l, 0.0)
    s2 = jnp.where((d < half) & live, -sin_l, 0.0)
    return jnp.stack([c, s1, s2]), half


def _inproj_kernel(x_ref, w_ref, tab_ref, o_ref, xb_ref, *, variants, shifts):
    j = pl.program_id(1)

    @pl.when(j == 0)
    def _():
        xb_ref[...] = x_ref[...].astype(BF16)

    acc = jnp.dot(xb_ref[...], w_ref[...], preferred_element_type=F32)

    for (j_lo, j_hi), types in variants:
        @pl.when((j >= j_lo) & (j < j_hi))
        def _(types=types):
            for c, p in enumerate(types):
                blk = acc[:, c * LANES:(c + 1) * LANES]
                if p >= 0:
                    sh = shifts[p]
                    blk = (blk * tab_ref[3 * p]
                           + pltpu.roll(blk, sh, 1) * tab_ref[3 * p + 1]
                           + pltpu.roll(blk, LANES - sh, 1) * tab_ref[3 * p + 2])
                o_ref[:, c * LANES:(c + 1) * LANES] = blk.astype(o_ref.dtype)


def _inproj(x, w, tabs, shifts, chunk_types, out_dtype, *, tm, tn, name):
    T, K = x.shape
    N = w.shape[1]
    tm = min(tm, T)
    assert T % tm == 0 and N % tn == 0 and len(chunk_types) == N // LANES
    per = tn // LANES
    tile_types = [tuple(chunk_types[t * per:(t + 1) * per]) for t in range(N // tn)]
    variants = []
    for t, ty in enumerate(tile_types):
        if variants and variants[-1][1] == ty and variants[-1][0][1] == t:
            variants[-1] = ((variants[-1][0][0], t + 1), ty)
        else:
            variants.append(((t, t + 1), ty))
    if tabs is None:
        tabs = jnp.zeros((3, T, LANES), F32)
    P3 = tabs.shape[0]
    return pl.pallas_call(
        functools.partial(_inproj_kernel, variants=tuple(variants), shifts=tuple(shifts)),
        grid=(T // tm, N // tn),
        in_specs=[pl.BlockSpec((tm, K), lambda i, j: (i, 0)),
                  pl.BlockSpec((K, tn), lambda i, j: (0, j)),
                  pl.BlockSpec((P3, tm, LANES), lambda i, j: (0, i, 0))],
        out_specs=pl.BlockSpec((tm, tn), lambda i, j: (i, j)),
        out_shape=jax.ShapeDtypeStruct((T, N), out_dtype),
        scratch_shapes=[pltpu.VMEM((tm, K), BF16)],
        compiler_params=_cparams(("parallel", "arbitrary")),
        name=name,
    )(x, w, tabs)


def _proj_ln_kernel(h_ref, w_ref, x_ref, g_ref, b_ref, o_ref, acc_ref):
    k = pl.program_id(1)

    @pl.when(k == 0)
    def _():
        acc_ref[...] = jnp.zeros_like(acc_ref)

    acc_ref[...] += jnp.dot(h_ref[...], w_ref[...], preferred_element_type=F32)

    @pl.when(k == pl.num_programs(1) - 1)
    def _():
        y = ALPHA * x_ref[...] + acc_ref[...]
        o_ref[...] = _layer_norm_rows(y, g_ref[...], b_ref[...])


def _proj_ln(h, w, x, g, b, *, tm=512, tk=512, name):
    T, K = h.shape
    D = w.shape[1]
    tm = min(tm, T)
    assert T % tm == 0 and K % tk == 0
    return pl.pallas_call(
        _proj_ln_kernel,
        grid=(T // tm, K // tk),
        in_specs=[pl.BlockSpec((tm, tk), lambda i, k: (i, k)),
                  pl.BlockSpec((tk, D), lambda i, k: (k, 0)),
                  pl.BlockSpec((tm, D), lambda i, k: (i, 0)),
                  pl.BlockSpec((1, D), lambda i, k: (0, 0)),
                  pl.BlockSpec((1, D), lambda i, k: (0, 0))],
        out_specs=pl.BlockSpec((tm, D), lambda i, k: (i, 0)),
        out_shape=jax.ShapeDtypeStruct((T, D), F32),
        scratch_shapes=[pltpu.VMEM((tm, D), F32)],
        compiler_params=_cparams(("parallel", "arbitrary")),
        name=name,
    )(h, w, x, g.reshape(1, D), b.reshape(1, D))


def _ffn_kernel(x_ref, wg_ref, wu_ref, wo_ref, g_ref, b_ref, o_ref, xb_ref, acc_ref):
    f = pl.program_id(1)

    @pl.when(f == 0)
    def _():
        xb_ref[...] = x_ref[...].astype(BF16)
        acc_ref[...] = jnp.zeros_like(acc_ref)

    xb = xb_ref[...]
    gate = jnp.dot(xb, wg_ref[...], preferred_element_type=F32)
    up = jnp.dot(xb, wu_ref[...], preferred_element_type=F32)
    h = (gate * jax.nn.sigmoid(gate) * up).astype(BF16)
    acc_ref[...] += jnp.dot(h, wo_ref[...], preferred_element_type=F32)

    @pl.when(f == pl.num_programs(1) - 1)
    def _():
        y = ALPHA * x_ref[...] + acc_ref[...]
        o_ref[...] = _layer_norm_rows(y, g_ref[...], b_ref[...])


def _ffn(x, w_in, w_out, g, b, *, tm=512, tf=512, name):
    T, D = x.shape
    dff = w_out.shape[0]
    tm = min(tm, T)
    assert T % tm == 0 and dff % tf == 0
    nf = dff // tf
    return pl.pallas_call(
        _ffn_kernel,
        grid=(T // tm, nf),
        in_specs=[pl.BlockSpec((tm, D), lambda i, f: (i, 0)),
                  pl.BlockSpec((D, tf), lambda i, f: (0, f)),
                  pl.BlockSpec((D, tf), lambda i, f: (0, f + nf)),
                  pl.BlockSpec((tf, D), lambda i, f: (f, 0)),
                  pl.BlockSpec((1, D), lambda i, f: (0, 0)),
                  pl.BlockSpec((1, D), lambda i, f: (0, 0))],
        out_specs=pl.BlockSpec((tm, D), lambda i, f: (i, 0)),
        out_shape=jax.ShapeDtypeStruct((T, D), F32),
        scratch_shapes=[pltpu.VMEM((tm, D), BF16), pltpu.VMEM((tm, D), F32)],
        compiler_params=_cparams(("parallel", "arbitrary")),
        name=name,
    )(x, w_in, w_in, w_out, g.reshape(1, D), b.reshape(1, D))


def _xattn_kernel(x_ref, wq_ref, k_ref, v_ref, o_ref):
    xb = x_ref[...].astype(BF16)
    q = jnp.dot(xb, wq_ref[...], preferred_element_type=F32).astype(BF16)
    scale = XA_HEAD_DIM ** -0.5
    for h in range(XA_HEADS):
        sl = slice(h * XA_HEAD_DIM, (h + 1) * XA_HEAD_DIM)
        s = lax.dot_general(q[:, sl], k_ref[:, sl], (((1,), (1,)), ((), ())),
                            preferred_element_type=F32) * scale
        m = jnp.max(s, axis=-1, keepdims=True)
        p = jnp.exp(s - m)
        l = jnp.sum(p, axis=-1, keepdims=True)
        p = (p / l).astype(BF16)
        o = jnp.dot(p, v_ref[:, sl], preferred_element_type=F32)
        o_ref[:, sl] = o.astype(o_ref.dtype)


def _xattn(x, wq, kv, batch, *, tm=512, name):
    T, D = x.shape
    S = T // batch
    M = kv.shape[0] // batch
    tm = min(tm, S)
    ns = S // tm
    return pl.pallas_call(
        _xattn_kernel,
        grid=(batch, ns),
        in_specs=[pl.BlockSpec((tm, D), lambda b, i: (b * ns + i, 0)),
                  pl.BlockSpec((D, D), lambda b, i: (0, 0)),
                  pl.BlockSpec((M, D), lambda b, i: (b, 0)),
                  pl.BlockSpec((M, D), lambda b, i: (b, 1))],
        out_specs=pl.BlockSpec((tm, D), lambda b, i: (b * ns + i, 0)),
        out_shape=jax.ShapeDtypeStruct((T, D), BF16),
        compiler_params=_cparams(("parallel", "arbitrary")),
        name=name,
    )(x, wq, kv, kv)


def _causal_pairs(n):
    qi = [i for i in range(n) for _ in range(i + 1)]
    kj = [j for i in range(n) for j in range(i + 1)]
    return jnp.asarray(qi, jnp.int32), jnp.asarray(kj, jnp.int32)


def _diff_attn_kernel(qi_ref, kj_ref, lam_ref, q_ref, k_ref, v_ref, g_ref, o_ref,
                      m_ref, l_ref, acc_ref, *, tq, lam_init):
    p_id = pl.program_id(2)
    qi = qi_ref[p_id]
    kj = kj_ref[p_id]
    scale = DIFF_HEAD_DIM ** -0.5

    @pl.when(kj == 0)
    def _():
        m_ref[...] = jnp.full_like(m_ref, -jnp.inf)
        l_ref[...] = jnp.zeros_like(l_ref)
        acc_ref[...] = jnp.zeros_like(acc_ref)

    def step(masked):
        q = q_ref[...]
        k = k_ref[...]
        v = v_ref[...]
        lane = lax.broadcasted_iota(jnp.int32, q.shape, 1)
        zero = jnp.zeros_like(q)
        q_comp = (jnp.where(lane < DIFF_HEAD_DIM, q, zero), jnp.where(lane >= DIFF_HEAD_DIM, q, zero))
        if masked:
            row = lax.broadcasted_iota(jnp.int32, (tq, tq), 0)
            col = lax.broadcasted_iota(jnp.int32, (tq, tq), 1)
            keep = col <= row
        for c in range(2):
            s = lax.dot_general(q_comp[c], k, (((1,), (1,)), ((), ())),
                                preferred_element_type=F32) * scale
            if masked:
                s = jnp.where(keep, s, NEG)
            m_prev = m_ref[c]
            m_new = jnp.maximum(m_prev, jnp.max(s, axis=-1, keepdims=True))
            alpha = jnp.exp(m_prev - m_new)
            p = jnp.exp(s - m_new)
            l_ref[c] = alpha * l_ref[c] + jnp.sum(p, axis=-1, keepdims=True)
            acc_ref[c] = alpha * acc_ref[c] + jnp.dot(p.astype(BF16), v, preferred_element_type=F32)
            m_ref[c] = m_new

    @pl.when(kj < qi)
    def _():
        step(False)

    @pl.when(kj == qi)
    def _():
        step(True)
        lam = lam_ref[0]
        a = acc_ref[0] / l_ref[0] - lam * (acc_ref[1] / l_ref[1])
        ms = jnp.mean(a * a, axis=-1, keepdims=True)
        a = a * lax.rsqrt(ms + LN_EPS) * g_ref[...]
        o_ref[...] = (a * (1.0 - lam_init)).astype(o_ref.dtype)


def _diff_attn(proj, lam, subln_g, batch, lam_init, *, tq=512, name):
    T = proj.shape[0]
    S = T // batch
    tq = min(tq, S)
    n = S // tq
    qi, kj = _causal_pairs(n)
    H = DIFF_HEADS
    grid_spec = pltpu.PrefetchScalarGridSpec(
        num_scalar_prefetch=2,
        grid=(batch, H, int(qi.shape[0])),
        in_specs=[pl.BlockSpec(memory_space=pltpu.SMEM),
                  pl.BlockSpec((tq, LANES), lambda b, h, p, qi, kj: (b * n + qi[p], h)),
                  pl.BlockSpec((tq, LANES), lambda b, h, p, qi, kj: (b * n + kj[p], H + h)),
                  pl.BlockSpec((tq, LANES), lambda b, h, p, qi, kj: (b * n + kj[p], 2 * H + h)),
                  pl.BlockSpec((1, LANES), lambda b, h, p, qi, kj: (0, 0))],
        out_specs=pl.BlockSpec((tq, LANES), lambda b, h, p, qi, kj: (b * n + qi[p], h)),
        scratch_shapes=[pltpu.VMEM((2, tq, 1), F32), pltpu.VMEM((2, tq, 1), F32),
                        pltpu.VMEM((2, tq, LANES), F32)])
    return pl.pallas_call(
        functools.partial(_diff_attn_kernel, tq=tq, lam_init=lam_init),
        grid_spec=grid_spec,
        out_shape=jax.ShapeDtypeStruct((T, DIFF_WIDTH), BF16),
        compiler_params=_cparams(("parallel", "parallel", "arbitrary")),
        name=name,
    )(qi, kj, lam.reshape(1).astype(F32), proj, proj, proj, subln_g.reshape(1, LANES).astype(F32))


CONV_HALO = 32
CONV_ROWS = 64


def _conv_kernel(val_ref, gate_ref, hval_ref, hgate_ref, cw_ref, cb_ref, g_ref, b_ref, o_ref,
                 u_ref, c_ref, *, tr):
    i = pl.program_id(1)
    u_ref[CONV_HALO:, :] = val_ref[...].astype(F32) * jax.nn.sigmoid(gate_ref[...].astype(F32))
    halo = hval_ref[...].astype(F32) * jax.nn.sigmoid(hgate_ref[...].astype(F32))
    u_ref[:CONV_HALO, :] = jnp.where(i > 0, halo, 0.0)
    base = CONV_HALO - (CONV_WIDTH - 1)
    for r in range(tr // CONV_ROWS):
        for c in range(CONV_CH // LANES):
            cs = slice(c * LANES, (c + 1) * LANES)
            acc = jnp.broadcast_to(cb_ref[:, cs], (CONV_ROWS, LANES))
            for w in range(CONV_WIDTH):
                r0 = r * CONV_ROWS + base + w
                acc = acc + u_ref[r0:r0 + CONV_ROWS, cs] * cw_ref[w:w + 1, cs]
            c_ref[r * CONV_ROWS:(r + 1) * CONV_ROWS, cs] = acc
    y = _layer_norm_rows(c_ref[...], g_ref[...], b_ref[...])
    o_ref[...] = (y * jax.nn.sigmoid(y)).astype(o_ref.dtype)


def _conv_module(proj, conv_w, conv_b, ln_g, ln_b, batch, *, tr=256, name):
    T = proj.shape[0]
    S = T // batch
    tr = min(tr, S)
    ns = S // tr
    vb = (3 * DIFF_WIDTH) // CONV_CH
    hpb = tr // CONV_HALO

    def halo_idx(col):
        return lambda b, i: (jnp.maximum((b * ns + i) * hpb - 1, 0), col)

    vec = lambda a: a.reshape(1, CONV_CH).astype(F32)
    return pl.pallas_call(
        functools.partial(_conv_kernel, tr=tr),
        grid=(batch, ns),
        in_specs=[pl.BlockSpec((tr, CONV_CH), lambda b, i: (b * ns + i, vb)),
                  pl.BlockSpec((tr, CONV_CH), lambda b, i: (b * ns + i, vb + 1)),
                  pl.BlockSpec((CONV_HALO, CONV_CH), halo_idx(vb)),
                  pl.BlockSpec((CONV_HALO, CONV_CH), halo_idx(vb + 1)),
                  pl.BlockSpec((CONV_WIDTH, CONV_CH), lambda b, i: (0, 0)),
                  pl.BlockSpec((1, CONV_CH), lambda b, i: (0, 0)),
                  pl.BlockSpec((1, CONV_CH), lambda b, i: (0, 0)),
                  pl.BlockSpec((1, CONV_CH), lambda b, i: (0, 0))],
        out_specs=pl.BlockSpec((tr, CONV_CH), lambda b, i: (b * ns + i, 0)),
        out_shape=jax.ShapeDtypeStruct((T, CONV_CH), BF16),
        scratch_shapes=[pltpu.VMEM((CONV_HALO + tr, CONV_CH), F32), pltpu.VMEM((tr, CONV_CH), F32)],
        compiler_params=_cparams(("parallel", "arbitrary")),
        name=name,
    )(proj, proj, proj, proj, conv_w.astype(F32), vec(conv_b), vec(ln_g), vec(ln_b))


IDX_CHUNK = 512
KEY_LO = -2139095040
KEY_HI = 0x7F800000


def _key_to_f32(key):
    bits = jnp.where(key < 0, key ^ jnp.int32(0x7FFFFFFF), key)
    return pltpu.bitcast(bits, F32)


def _indexer_kernel(q_ref, k_ref, o_ref, sc_ref, *, tq, topk, seq):
    i = pl.program_id(1)
    q0 = i * tq
    nch = (q0 + tq + IDX_CHUNK - 1) // IDX_CHUNK
    n_all = seq // IDX_CHUNK
    lane = lax.broadcasted_iota(jnp.int32, (tq, LANES), 1)

    qh = []
    for blk in range(IDX_Q // LANES):
        qb = q_ref[:, blk * LANES:(blk + 1) * LANES]
        qh.append(jnp.where(lane < IDX_DIM, qb, 0.0).astype(BF16))
        qh.append(jnp.where(lane < IDX_DIM, pltpu.roll(qb, IDX_DIM, 1), 0.0).astype(BF16))
    wcol = IDX_Q + IDX_DIM
    wscale = IDX_HEADS ** -0.5 * IDX_DIM ** -0.5
    wts = [q_ref[:, wcol + h:wcol + h + 1] * wscale for h in range(IDX_HEADS)]

    def score_chunk(c, masked):
        k0 = pl.multiple_of(c * IDX_CHUNK, IDX_CHUNK)
        kc = k_ref[pl.ds(k0, IDX_CHUNK), :]
        sc = None
        for h in range(IDX_HEADS):
            r = lax.dot_general(qh[h], kc, (((1,), (1,)), ((), ())), preferred_element_type=F32)
            t = wts[h] * jnp.maximum(r, 0.0)
            sc = t if sc is None else sc + t
        if masked:
            row = q0 + lax.broadcasted_iota(jnp.int32, (tq, IDX_CHUNK), 0)
            col = k0 + lax.broadcasted_iota(jnp.int32, (tq, IDX_CHUNK), 1)
            sc = jnp.where(col <= row, sc, -jnp.inf)
        sc_ref[:, pl.ds(k0, IDX_CHUNK)] = sc

    def unmasked_body(c, carry):
        score_chunk(c, False)
        return carry

    lax.fori_loop(0, nch - 1, unmasked_body, 0)
    score_chunk(nch - 1, True)

    def count(pred):
        def body(c, acc):
            k0 = pl.multiple_of(c * IDX_CHUNK, IDX_CHUNK)
            for jj in range(IDX_CHUNK // LANES):
                blk = sc_ref[:, pl.ds(k0 + jj * LANES, LANES)]
                acc = acc + jnp.where(pred(blk), 1.0, 0.0)
            return acc
        acc = lax.fori_loop(0, nch, body, jnp.zeros((tq, LANES), F32))
        return jnp.broadcast_to(jnp.sum(acc, axis=1, keepdims=True), (tq, LANES))

    def bisect(_, carry):
        lo, hi = carry
        mid = (lo >> 1) + (hi >> 1) + (lo & hi & 1)
        cand = _key_to_f32(mid)
        ok = count(lambda blk: blk >= cand) >= float(topk)
        return jnp.where(ok, mid, lo), jnp.where(ok, hi, mid)

    lo0 = jnp.full((tq, LANES), KEY_LO, jnp.int32)
    hi0 = jnp.full((tq, LANES), KEY_HI, jnp.int32)
    lo, _ = lax.fori_loop(0, 32, bisect, (lo0, hi0))
    thr = _key_to_f32(lo)
    n_gt = count(lambda blk: blk > thr)
    n_ge = count(lambda blk: blk >= thr)
    need = float(topk) - n_gt
    has_tie = jnp.max(jnp.where(n_ge > float(topk), 1.0, 0.0)) > 0.0
    thr_c = thr[:, :1]

    @pl.when(jnp.logical_not(has_tie))
    def _():
        def body(c, carry):
            k0 = pl.multiple_of(c * IDX_CHUNK, IDX_CHUNK)
            sc = sc_ref[:, pl.ds(k0, IDX_CHUNK)]
            o_ref[0, :, pl.ds(k0, IDX_CHUNK)] = jnp.where(sc >= thr_c, 0.0, NEG).astype(o_ref.dtype)
            return carry
        lax.fori_loop(0, nch, body, 0)

    @pl.when(has_tie)
    def _():
        r_i = lax.broadcasted_iota(jnp.int32, (IDX_CHUNK, IDX_CHUNK), 0)
        c_i = lax.broadcasted_iota(jnp.int32, (IDX_CHUNK, IDX_CHUNK), 1)
        upper = jnp.where(r_i < c_i, 1.0, 0.0).astype(BF16)
        need_c = need[:, :1]

        def body(c, seen):
            k0 = pl.multiple_of(c * IDX_CHUNK, IDX_CHUNK)
            sc = sc_ref[:, pl.ds(k0, IDX_CHUNK)]
            eq = jnp.where(sc == thr_c, 1.0, 0.0)
            before = seen + jnp.dot(eq.astype(BF16), upper, preferred_element_type=F32)
            keep = (sc > thr_c) | ((sc == thr_c) & (before < need_c))
            o_ref[0, :, pl.ds(k0, IDX_CHUNK)] = jnp.where(keep, 0.0, NEG).astype(o_ref.dtype)
            return seen + jnp.sum(eq, axis=1, keepdims=True)
        lax.fori_loop(0, nch, body, jnp.zeros((tq, 1), F32))

    def fill(c, carry):
        k0 = pl.multiple_of(c * IDX_CHUNK, IDX_CHUNK)
        o_ref[0, :, pl.ds(k0, IDX_CHUNK)] = jnp.full((tq, IDX_CHUNK), NEG, o_ref.dtype)
        return carry
    lax.fori_loop(nch, n_all, fill, 0)


def _indexer(idx, kidx, batch, *, tq=128, name):
    T = idx.shape[0]
    S = T // batch
    tq = min(tq, S)
    ns = S // tq
    topk = min(TOPK_MAX, S // 4)
    assert S % IDX_CHUNK == 0 and IDX_CHUNK % tq == 0
    return pl.pallas_call(
        functools.partial(_indexer_kernel, tq=tq, topk=topk, seq=S),
        grid=(batch, ns),
        in_specs=[pl.BlockSpec((tq, IDX_PAD), lambda b, i: (b * ns + i, 0)),
                  pl.BlockSpec((S, LANES), lambda b, i: (b, 0))],
        out_specs=pl.BlockSpec((1, tq, S), lambda b, i: (b, i, 0)),
        out_shape=jax.ShapeDtypeStruct((batch, S, S), BF16),
        scratch_shapes=[pltpu.VMEM((tq, S), F32)],
        compiler_params=_cparams(("parallel", "arbitrary")),
        name=name,
    )(idx, kidx)


def _dsa_attn_kernel(qi_ref, kj_ref, q_ref, k_ref, v_ref, bias_ref, o_ref, m_ref, l_ref, acc_ref):
    p_id = pl.program_id(2)
    qi = qi_ref[p_id]
    kj = kj_ref[p_id]
    scale = DSA_HEAD_DIM ** -0.5

    @pl.when(kj == 0)
    def _():
        m_ref[...] = jnp.full_like(m_ref, -jnp.inf)
        l_ref[...] = jnp.zeros_like(l_ref)
        acc_ref[...] = jnp.zeros_like(acc_ref)

    k = k_ref[...]
    v = v_ref[...]
    bias = bias_ref[0].astype(F32)
    for r in range(DSA_REP):
        q = q_ref[:, r * DSA_HEAD_DIM:(r + 1) * DSA_HEAD_DIM]
        s = lax.dot_general(q, k, (((1,), (1,)), ((), ())), preferred_element_type=F32) * scale + bias
        m_prev = m_ref[r]
        m_new = jnp.maximum(m_prev, jnp.max(s, axis=-1, keepdims=True))
        alpha = jnp.exp(m_prev - m_new)
        p = jnp.exp(s - m_new)
        l_ref[r] = alpha * l_ref[r] + jnp.sum(p, axis=-1, keepdims=True)
        acc_ref[r] = alpha * acc_ref[r] + jnp.dot(p.astype(BF16), v, preferred_element_type=F32)
        m_ref[r] = m_new

    @pl.when(kj == qi)
    def _():
        for r in range(DSA_REP):
            o_ref[:, r * DSA_HEAD_DIM:(r + 1) * DSA_HEAD_DIM] = (acc_ref[r] / l_ref[r]).astype(o_ref.dtype)


def _dsa_attn(qkv, bias, batch, *, tq=512, name):
    T = qkv.shape[0]
    S = T // batch
    tq = min(tq, S)
    n = S // tq
    qi, kj = _causal_pairs(n)
    G = DSA_KV_HEADS
    gw = DSA_REP * DSA_HEAD_DIM
    kb = DSA_Q // DSA_HEAD_DIM
    grid_spec = pltpu.PrefetchScalarGridSpec(
        num_scalar_prefetch=2,
        grid=(batch, G, int(qi.shape[0])),
        in_specs=[pl.BlockSpec((tq, gw), lambda b, g, p, qi, kj: (b * n + qi[p], g)),
                  pl.BlockSpec((tq, DSA_HEAD_DIM), lambda b, g, p, qi, kj: (b * n + kj[p], kb + g)),
                  pl.BlockSpec((tq, DSA_HEAD_DIM), lambda b, g, p, qi, kj: (b * n + kj[p], kb + G + g)),
                  pl.BlockSpec((1, tq, tq), lambda b, g, p, qi, kj: (b, qi[p], kj[p]))],
        out_specs=pl.BlockSpec((tq, gw), lambda b, g, p, qi, kj: (b * n + qi[p], g)),
        scratch_shapes=[pltpu.VMEM((DSA_REP, tq, 1), F32), pltpu.VMEM((DSA_REP, tq, 1), F32),
                        pltpu.VMEM((DSA_REP, tq, DSA_HEAD_DIM), F32)])
    return pl.pallas_call(
        _dsa_attn_kernel,
        grid_spec=grid_spec,
        out_shape=jax.ShapeDtypeStruct((T, DSA_Q), BF16),
        compiler_params=_cparams(("parallel", "parallel", "arbitrary")),
        name=name,
    )(qi, kj, qkv, qkv, qkv, bias)


def _even_mixer(x, positions, batch, w_in, w_out, lam_p, subln_g, conv_w, conv_b, conv_ln_g, conv_ln_b,
                lam_init, ln_g, ln_b):
    tabs, half = _rot_tables(positions, DIFF_HEAD_DIM)
    n_rot = (2 * DIFF_WIDTH) // LANES
    chunk_types = [0] * n_rot + [-1] * ((EVEN_IN - 2 * DIFF_WIDTH) // LANES)
    proj = _inproj(x, w_in.astype(BF16), tabs, (half,), chunk_types, BF16, tm=1024, tn=512, name="even_inproj")
    lp = lam_p.astype(F32)
    lam = jnp.exp(jnp.sum(lp[0] * lp[1])) - jnp.exp(jnp.sum(lp[2] * lp[3])) + lam_init
    a = _diff_attn(proj, lam, subln_g, batch, lam_init, name="diff_attn")
    c = _conv_module(proj, conv_w, conv_b, conv_ln_g, conv_ln_b, batch, name="conv_module")
    h = jnp.concatenate([a, c], axis=-1)
    return _proj_ln(h, w_out.astype(BF16), x, ln_g, ln_b, name="even_outproj_ln")


def _odd_mixer(x, positions, batch, w_in, w_out, ln_g, ln_b):
    T = x.shape[0]
    tabs_b, half_b = _rot_tables(positions, DSA_HEAD_DIM)
    n_rot = (DSA_Q + DSA_KV) // LANES
    chunk_types = [0] * n_rot + [-1] * (DSA_KV // LANES)
    w_main = w_in[:, :ODD_MAIN].astype(BF16)
    qkv = _inproj(x, w_main, tabs_b, (half_b,), chunk_types, BF16, tm=1024, tn=512, name="odd_inproj")
    w_idx = jnp.pad(w_in[:, ODD_MAIN:], ((0, 0), (0, IDX_PAD - (w_in.shape[1] - ODD_MAIN)))).astype(BF16)
    tabs_a, half_a = _rot_tables(positions, IDX_DIM)
    tabs_ah, _ = _rot_tables(positions, IDX_DIM, active_lanes=IDX_DIM)
    idx_types = [0] * (IDX_Q // LANES) + [1] + [-1] * ((IDX_PAD - IDX_Q) // LANES - 1)
    idx = _inproj(x, w_idx, jnp.concatenate([tabs_a, tabs_ah]), (half_a, half_a), idx_types, F32,
                  tm=1024, tn=IDX_PAD, name="idx_inproj")
    kidx = idx[:, IDX_Q:IDX_Q + LANES].astype(BF16)
    bias = _indexer(idx, kidx, batch, name="indexer")
    o = _dsa_attn(qkv, bias, batch, name="dsa_attn")
    return _proj_ln(o, w_out.astype(BF16), x, ln_g, ln_b, name="odd_outproj_ln")


def kernel(x, mem, positions, w_in_even, w_out_even, diff_lambda, diff_subln_g, conv_w, conv_b, conv_ln_g,
           conv_ln_b, w_in_odd, w_out_odd, xa_wq, xa_wkv, xa_wo, ffn_w_in, ffn_w_out, ln_g, ln_b):
    B, S, D = x.shape
    M = mem.shape[1]
    x = x.reshape(B * S, D)
    mem2 = mem.reshape(B * M, D)
    for layer in range(DEPTH):
        j = layer // 2
        if layer % 2 == 0:
            lam_init = 0.8 - 0.6 * math.exp(-0.3 * layer)
            x = _even_mixer(x, positions, B, w_in_even[j], w_out_even[j], diff_lambda[j], diff_subln_g[j],
                            conv_w[j], conv_b[j], conv_ln_g[j], conv_ln_b[j], lam_init,
                            ln_g[layer, 0], ln_b[layer, 0])
        else:
            x = _odd_mixer(x, positions, B, w_in_odd[j], w_out_odd[j], ln_g[layer, 0], ln_b[layer, 0])
        kv = _inproj(mem2, xa_wkv[layer].astype(BF16), None, (), [-1] * (2 * D // LANES), BF16,
                     tm=512, tn=512, name=f"xa_kvproj_{layer}")
        o = _xattn(x, xa_wq[layer].astype(BF16), kv, B, name=f"xattn_{layer}")
        x = _proj_ln(o, xa_wo[layer].astype(BF16), x, ln_g[layer, 1], ln_b[layer, 1], name=f"xa_outproj_ln_{layer}")
        x = _ffn(x, ffn_w_in[layer].astype(BF16), ffn_w_out[layer].astype(BF16),
                 ln_g[layer, 2], ln_b[layer, 2], name=f"ffn_{layer}")
    return x.reshape(B, S, D)
```

```python
import functools
import math

import jax
import jax.numpy as jnp
from jax import lax
from jax.experimental import pallas as pl
from jax.experimental.pallas import tpu as pltpu

F32 = jnp.float32
BF16 = jnp.bfloat16

D_MODEL = 2048
DEPTH = 2
ALPHA = (2 * DEPTH) ** 0.25
LN_EPS = 1e-5
ROPE_THETA = 500000.0
ROPE_FRAC = 4

DIFF_HEADS = 8
DIFF_HEAD_DIM = 64
DIFF_V_DIM = 2 * DIFF_HEAD_DIM
DIFF_WIDTH = DIFF_HEADS * DIFF_V_DIM
CONV_CH = D_MODEL - DIFF_WIDTH
CONV_WIDTH = 31
EVEN_IN = 3 * DIFF_WIDTH + 2 * CONV_CH

DSA_HEADS = 16
DSA_KV_HEADS = 4
DSA_HEAD_DIM = 128
DSA_REP = DSA_HEADS // DSA_KV_HEADS
IDX_HEADS = 4
IDX_DIM = 64
TOPK_MAX = 256
DSA_Q = DSA_HEADS * DSA_HEAD_DIM
DSA_KV = DSA_KV_HEADS * DSA_HEAD_DIM
IDX_Q = IDX_HEADS * IDX_DIM
ODD_MAIN = DSA_Q + 2 * DSA_KV
IDX_PAD = 512

XA_HEADS = 4
XA_HEAD_DIM = D_MODEL // XA_HEADS
D_FF = -(-(8 * D_MODEL) // (3 * 256)) * 256

LANES = 128
NEG = -1e30
VMEM_LIMIT = 56 * 1024 * 1024


def _cparams(sem):
    return pltpu.CompilerParams(dimension_semantics=sem, vmem_limit_bytes=VMEM_LIMIT)


def _layer_norm_rows(y, g, b):
    mu = jnp.mean(y, axis=-1, keepdims=True)
    d = y - mu
    var = jnp.mean(d * d, axis=-1, keepdims=True)
    return d * lax.rsqrt(var + LN_EPS) * g + b


def _rot_tables(positions, head_dim, active_lanes=LANES):
    rot = head_dim // ROPE_FRAC
    half = rot // 2
    inv_freq = 1.0 / (ROPE_THETA ** (jnp.arange(half, dtype=F32) / half))
    ang = positions.reshape(-1, 1).astype(F32) * inv_freq
    cos, sin = jnp.cos(ang), jnp.sin(ang)
    lane = jnp.arange(LANES)
    d = lane % head_dim
    f = d % half
    live = lane < active_lanes
    cos_l = jnp.take(cos, f, axis=1)
    sin_l = jnp.take(sin, f, axis=1)
    c = jnp.where((d < rot) & live, cos_l, 1.0)
    s1 = jnp.where((d >= half) & (d < rot) & live, sin_l, 0.0)
    s2 = jnp.where((d < half) & live, -sin_l, 0.0)
    return jnp.stack([c, s1, s2]), half


def _inproj_kernel(x_ref, w_ref, tab_ref, o_ref, xb_ref, *, variants, shifts):
    j = pl.program_id(1)

    @pl.when(j == 0)
    def _():
        xb_ref[...] = x_ref[...].astype(BF16)

    acc = jnp.dot(xb_ref[...], w_ref[...], preferred_element_type=F32)

    for (j_lo, j_hi), types in variants:
        @pl.when((j >= j_lo) & (j < j_hi))
        def _(types=types):
            for c, (p, scale) in enumerate(types):
                blk = acc[:, c * LANES:(c + 1) * LANES]
                if p >= 0:
                    sh = shifts[p]
                    blk = (blk * tab_ref[3 * p]
                           + pltpu.roll(blk, sh, 1) * tab_ref[3 * p + 1]
                           + pltpu.roll(blk, LANES - sh, 1) * tab_ref[3 * p + 2])
                if scale != 1.0:
                    blk = blk * scale
                o_ref[:, c * LANES:(c + 1) * LANES] = blk.astype(o_ref.dtype)


def _inproj(x, w, tabs, shifts, chunk_types, out_dtype, *, tm, tn, name, chunk_scales=None):
    T, K = x.shape
    N = w.shape[1]
    tm = min(tm, T)
    assert T % tm == 0 and N % tn == 0 and len(chunk_types) == N // LANES
    if chunk_scales is None:
        chunk_scales = [1.0] * len(chunk_types)
    chunk_types = list(zip(chunk_types, chunk_scales))
    per = tn // LANES
    tile_types = [tuple(chunk_types[t * per:(t + 1) * per]) for t in range(N // tn)]
    variants = []
    for t, ty in enumerate(tile_types):
        if variants and variants[-1][1] == ty and variants[-1][0][1] == t:
            variants[-1] = ((variants[-1][0][0], t + 1), ty)
        else:
            variants.append(((t, t + 1), ty))
    if tabs is None:
        tabs = jnp.zeros((3, T, LANES), F32)
    P3 = tabs.shape[0]
    return pl.pallas_call(
        functools.partial(_inproj_kernel, variants=tuple(variants), shifts=tuple(shifts)),
        grid=(T // tm, N // tn),
        in_specs=[pl.BlockSpec((tm, K), lambda i, j: (i, 0)),
                  pl.BlockSpec((K, tn), lambda i, j: (0, j)),
                  pl.BlockSpec((P3, tm, LANES), lambda i, j: (0, i, 0))],
        out_specs=pl.BlockSpec((tm, tn), lambda i, j: (i, j)),
        out_shape=jax.ShapeDtypeStruct((T, N), out_dtype),
        scratch_shapes=[pltpu.VMEM((tm, K), BF16)],
        compiler_params=_cparams(("parallel", "arbitrary")),
        name=name,
    )(x, w, tabs)


def _proj_ln_kernel(h_ref, w_ref, x_ref, g_ref, b_ref, o_ref, acc_ref):
    k = pl.program_id(1)

    @pl.when(k == 0)
    def _():
        acc_ref[...] = jnp.zeros_like(acc_ref)

    acc_ref[...] += jnp.dot(h_ref[...], w_ref[...], preferred_element_type=F32)

    @pl.when(k == pl.num_programs(1) - 1)
    def _():
        y = ALPHA * x_ref[...] + acc_ref[...]
        o_ref[...] = _layer_norm_rows(y, g_ref[...], b_ref[...])


def _proj_ln(h, w, x, g, b, *, tm=512, tk=512, name):
    T, K = h.shape
    D = w.shape[1]
    tm = min(tm, T)
    assert T % tm == 0 and K % tk == 0
    return pl.pallas_call(
        _proj_ln_kernel,
        grid=(T // tm, K // tk),
        in_specs=[pl.BlockSpec((tm, tk), lambda i, k: (i, k)),
                  pl.BlockSpec((tk, D), lambda i, k: (k, 0)),
                  pl.BlockSpec((tm, D), lambda i, k: (i, 0)),
                  pl.BlockSpec((1, D), lambda i, k: (0, 0)),
                  pl.BlockSpec((1, D), lambda i, k: (0, 0))],
        out_specs=pl.BlockSpec((tm, D), lambda i, k: (i, 0)),
        out_shape=jax.ShapeDtypeStruct((T, D), F32),
        scratch_shapes=[pltpu.VMEM((tm, D), F32)],
        compiler_params=_cparams(("parallel", "arbitrary")),
        name=name,
    )(h, w, x, g.reshape(1, D), b.reshape(1, D))


def _ffn_kernel(x_ref, wg_ref, wu_ref, wo_ref, g_ref, b_ref, o_ref, xb_ref, acc_ref):
    f = pl.program_id(1)

    @pl.when(f == 0)
    def _():
        xb_ref[...] = x_ref[...].astype(BF16)
        acc_ref[...] = jnp.zeros_like(acc_ref)

    xb = xb_ref[...]
    gate = jnp.dot(xb, wg_ref[...], preferred_element_type=F32)
    up = jnp.dot(xb, wu_ref[...], preferred_element_type=F32)
    h = (gate * jax.nn.sigmoid(gate) * up).astype(BF16)
    acc_ref[...] += jnp.dot(h, wo_ref[...], preferred_element_type=F32)

    @pl.when(f == pl.num_programs(1) - 1)
    def _():
        y = ALPHA * x_ref[...] + acc_ref[...]
        o_ref[...] = _layer_norm_rows(y, g_ref[...], b_ref[...])


def _ffn(x, w_in, w_out, g, b, *, tm=512, tf=512, name):
    T, D = x.shape
    dff = w_out.shape[0]
    tm = min(tm, T)
    assert T % tm == 0 and dff % tf == 0
    nf = dff // tf
    return pl.pallas_call(
        _ffn_kernel,
        grid=(T // tm, nf),
        in_specs=[pl.BlockSpec((tm, D), lambda i, f: (i, 0)),
                  pl.BlockSpec((D, tf), lambda i, f: (0, f)),
                  pl.BlockSpec((D, tf), lambda i, f: (0, f + nf)),
                  pl.BlockSpec((tf, D), lambda i, f: (f, 0)),
                  pl.BlockSpec((1, D), lambda i, f: (0, 0)),
                  pl.BlockSpec((1, D), lambda i, f: (0, 0))],
        out_specs=pl.BlockSpec((tm, D), lambda i, f: (i, 0)),
        out_shape=jax.ShapeDtypeStruct((T, D), F32),
        scratch_shapes=[pltpu.VMEM((tm, D), BF16), pltpu.VMEM((tm, D), F32)],
        compiler_params=_cparams(("parallel", "arbitrary")),
        name=name,
    )(x, w_in, w_in, w_out, g.reshape(1, D), b.reshape(1, D))


def _xattn_kernel(x_ref, wq_ref, k_ref, v_ref, o_ref):
    xb = x_ref[...].astype(BF16)
    q = jnp.dot(xb, wq_ref[...], preferred_element_type=F32).astype(BF16)
    scale = XA_HEAD_DIM ** -0.5
    for h in range(XA_HEADS):
        sl = slice(h * XA_HEAD_DIM, (h + 1) * XA_HEAD_DIM)
        s = lax.dot_general(q[:, sl], k_ref[:, sl], (((1,), (1,)), ((), ())),
                            preferred_element_type=F32) * scale
        m = jnp.max(s, axis=-1, keepdims=True)
        p = jnp.exp(s - m)
        l = jnp.sum(p, axis=-1, keepdims=True)
        p = (p / l).astype(BF16)
        o = jnp.dot(p, v_ref[:, sl], preferred_element_type=F32)
        o_ref[:, sl] = o.astype(o_ref.dtype)


def _xattn(x, wq, kv, batch, *, tm=512, name):
    T, D = x.shape
    S = T // batch
    M = kv.shape[0] // batch
    tm = min(tm, S)
    ns = S // tm
    return pl.pallas_call(
        _xattn_kernel,
        grid=(batch, ns),
        in_specs=[pl.BlockSpec((tm, D), lambda b, i: (b * ns + i, 0)),
                  pl.BlockSpec((D, D), lambda b, i: (0, 0)),
                  pl.BlockSpec((M, D), lambda b, i: (b, 0)),
                  pl.BlockSpec((M, D), lambda b, i: (b, 1))],
        out_specs=pl.BlockSpec((tm, D), lambda b, i: (b * ns + i, 0)),
        out_shape=jax.ShapeDtypeStruct((T, D), BF16),
        compiler_params=_cparams(("parallel", "arbitrary")),
        name=name,
    )(x, wq, kv, kv)


NT_DIMS = (((1,), (1,)), ((), ()))
LOG2E = math.log2(math.e)


def _causal_pairs(nq, ratio):
    qi = [i for i in range(nq) for _ in range((i + 1) * ratio)]
    kj = [j for i in range(nq) for j in range((i + 1) * ratio)]
    return jnp.asarray(qi, jnp.int32), jnp.asarray(kj, jnp.int32)


def _flash_update(q, k, v_ext, m_ref, acc_ref, idx, rows, bias=None, mask=None):
    s = lax.dot_general(q, k, NT_DIMS, preferred_element_type=F32)
    if bias is not None:
        s = s + bias
    if mask is not None:
        s = jnp.where(mask, s, NEG)
    chunks = [s[:, j * LANES:(j + 1) * LANES] for j in range(s.shape[1] // LANES)]
    mc = chunks[0]
    for ch in chunks[1:]:
        mc = jnp.maximum(mc, ch)
    m_prev = m_ref[idx, rows, :]
    m_new = jnp.maximum(m_prev, jnp.max(mc, axis=1, keepdims=True))
    p = jnp.concatenate([jnp.exp2(ch - m_new).astype(BF16) for ch in chunks], axis=1)
    alpha = jnp.exp2(m_prev - m_new)
    pv = jnp.dot(p, v_ext, preferred_element_type=F32)
    acc_ref[idx, rows, :] = jnp.concatenate([alpha, alpha], axis=1) * acc_ref[idx, rows, :] + pv
    m_ref[idx, rows, :] = m_new


def _flash_init(m_ref, acc_ref):
    m_ref[...] = jnp.full_like(m_ref, -jnp.inf)
    acc_ref[...] = jnp.zeros_like(acc_ref)


def _with_ones(v):
    return jnp.concatenate([v, jnp.ones_like(v)], axis=1)


def _diff_attn_kernel(qi_ref, kj_ref, lam_ref, q_ref, k_ref, v_ref, g_ref, o_ref, m_ref, acc_ref,
                      *, tq, tk, rq, lam_init):
    p_id = pl.program_id(2)
    qi = qi_ref[p_id]
    kj = kj_ref[p_id]
    ratio = tq // tk

    @pl.when(kj == 0)
    def _():
        _flash_init(m_ref, acc_ref)

    def step(d):
        k = k_ref[...]
        v_ext = _with_ones(v_ref[...])
        lane = lax.broadcasted_iota(jnp.int32, (rq, LANES), 1)
        for rb in range(tq // rq):
            if d is not None and (rb + 1) * rq <= d * tk:
                continue
            rows = slice(rb * rq, (rb + 1) * rq)
            q = q_ref[rows, :]
            mask = None
            if d is not None and rb * rq < (d + 1) * tk - 1:
                row = rb * rq + lax.broadcasted_iota(jnp.int32, (rq, tk), 0)
                col = d * tk + lax.broadcasted_iota(jnp.int32, (rq, tk), 1)
                mask = col <= row
            for c, sel in enumerate((lane < DIFF_HEAD_DIM, lane >= DIFF_HEAD_DIM)):
                _flash_update(jnp.where(sel, q, jnp.zeros_like(q)), k, v_ext, m_ref, acc_ref, c, rows, mask=mask)

    @pl.when(kj < qi * ratio)
    def _():
        step(None)

    for d in range(ratio):
        @pl.when(kj == qi * ratio + d)
        def _(d=d):
            step(d)

    @pl.when(kj == (qi + 1) * ratio - 1)
    def _():
        lam = lam_ref[0]
        a = (acc_ref[0, :, :LANES] / acc_ref[0, :, LANES:]
             - lam * (acc_ref[1, :, :LANES] / acc_ref[1, :, LANES:]))
        ms = jnp.mean(a * a, axis=-1, keepdims=True)
        a = a * lax.rsqrt(ms + LN_EPS) * g_ref[...]
        o_ref[...] = (a * (1.0 - lam_init)).astype(o_ref.dtype)


def _diff_attn(proj, lam, subln_g, batch, lam_init, *, tq=1024, tk=512, rq=128, name):
    T = proj.shape[0]
    S = T // batch
    tq, tk = min(tq, S), min(tk, S)
    rq = min(rq, tq)
    assert S % tq == 0 and tq % tk == 0 and tq % rq == 0
    nq, nk = S // tq, S // tk
    qi, kj = _causal_pairs(nq, tq // tk)
    H = DIFF_HEADS
    grid_spec = pltpu.PrefetchScalarGridSpec(
        num_scalar_prefetch=2,
        grid=(batch, H, int(qi.shape[0])),
        in_specs=[pl.BlockSpec(memory_space=pltpu.SMEM),
                  pl.BlockSpec((tq, LANES), lambda b, h, p, qi, kj: (b * nq + qi[p], h)),
                  pl.BlockSpec((tk, LANES), lambda b, h, p, qi, kj: (b * nk + kj[p], H + h)),
                  pl.BlockSpec((tk, LANES), lambda b, h, p, qi, kj: (b * nk + kj[p], 2 * H + h)),
                  pl.BlockSpec((1, LANES), lambda b, h, p, qi, kj: (0, 0))],
        out_specs=pl.BlockSpec((tq, LANES), lambda b, h, p, qi, kj: (b * nq + qi[p], h)),
        scratch_shapes=[pltpu.VMEM((2, tq, LANES), F32), pltpu.VMEM((2, tq, 2 * LANES), F32)])
    return pl.pallas_call(
        functools.partial(_diff_attn_kernel, tq=tq, tk=tk, rq=rq, lam_init=lam_init),
        grid_spec=grid_spec,
        out_shape=jax.ShapeDtypeStruct((T, DIFF_WIDTH), BF16),
        compiler_params=_cparams(("parallel", "parallel", "arbitrary")),
        name=name,
    )(qi, kj, lam.reshape(1).astype(F32), proj, proj, proj, subln_g.reshape(1, LANES).astype(F32))


CONV_HALO = 32
CONV_ROWS = 64


def _conv_kernel(val_ref, gate_ref, hval_ref, hgate_ref, cw_ref, cb_ref, g_ref, b_ref, o_ref,
                 u_ref, c_ref, *, tr):
    i = pl.program_id(1)
    u_ref[CONV_HALO:, :] = val_ref[...].astype(F32) * jax.nn.sigmoid(gate_ref[...].astype(F32))
    halo = hval_ref[...].astype(F32) * jax.nn.sigmoid(hgate_ref[...].astype(F32))
    u_ref[:CONV_HALO, :] = jnp.where(i > 0, halo, 0.0)
    base = CONV_HALO - (CONV_WIDTH - 1)
    for r in range(tr // CONV_ROWS):
        for c in range(CONV_CH // LANES):
            cs = slice(c * LANES, (c + 1) * LANES)
            acc = jnp.broadcast_to(cb_ref[:, cs], (CONV_ROWS, LANES))
            for w in range(CONV_WIDTH):
                r0 = r * CONV_ROWS + base + w
                acc = acc + u_ref[r0:r0 + CONV_ROWS, cs] * cw_ref[w:w + 1, cs]
            c_ref[r * CONV_ROWS:(r + 1) * CONV_ROWS, cs] = acc
    y = _layer_norm_rows(c_ref[...], g_ref[...], b_ref[...])
    o_ref[...] = (y * jax.nn.sigmoid(y)).astype(o_ref.dtype)


def _conv_module(proj, conv_w, conv_b, ln_g, ln_b, batch, *, tr=256, name):
    T = proj.shape[0]
    S = T // batch
    tr = min(tr, S)
    ns = S // tr
    vb = (3 * DIFF_WIDTH) // CONV_CH
    hpb = tr // CONV_HALO

    def halo_idx(col):
        return lambda b, i: (jnp.maximum((b * ns + i) * hpb - 1, 0), col)

    vec = lambda a: a.reshape(1, CONV_CH).astype(F32)
    return pl.pallas_call(
        functools.partial(_conv_kernel, tr=tr),
        grid=(batch, ns),
        in_specs=[pl.BlockSpec((tr, CONV_CH), lambda b, i: (b * ns + i, vb)),
                  pl.BlockSpec((tr, CONV_CH), lambda b, i: (b * ns + i, vb + 1)),
                  pl.BlockSpec((CONV_HALO, CONV_CH), halo_idx(vb)),
                  pl.BlockSpec((CONV_HALO, CONV_CH), halo_idx(vb + 1)),
                  pl.BlockSpec((CONV_WIDTH, CONV_CH), lambda b, i: (0, 0)),
                  pl.BlockSpec((1, CONV_CH), lambda b, i: (0, 0)),
                  pl.BlockSpec((1, CONV_CH), lambda b, i: (0, 0)),
                  pl.BlockSpec((1, CONV_CH), lambda b, i: (0, 0))],
        out_specs=pl.BlockSpec((tr, CONV_CH), lambda b, i: (b * ns + i, 0)),
        out_shape=jax.ShapeDtypeStruct((T, CONV_CH), BF16),
        scratch_shapes=[pltpu.VMEM((CONV_HALO + tr, CONV_CH), F32), pltpu.VMEM((tr, CONV_CH), F32)],
        compiler_params=_cparams(("parallel", "arbitrary")),
        name=name,
    )(proj, proj, proj, proj, conv_w.astype(F32), vec(conv_b), vec(ln_g), vec(ln_b))


IDX_CHUNK = 512
COUNT_CHUNK = 1024
KEY_LO = -2139095040
KEY_HI = 0x7F800000


def _key_to_f32(key):
    bits = jnp.where(key < 0, key ^ jnp.int32(0x7FFFFFFF), key)
    return pltpu.bitcast(bits, F32)


def _indexer_kernel(q_ref, k_ref, o_ref, sc_ref, *, tq, topk, seq):
    i = pl.program_id(1)
    q0 = i * tq
    nch = (q0 + tq + IDX_CHUNK - 1) // IDX_CHUNK
    n_all = seq // IDX_CHUNK
    lane = lax.broadcasted_iota(jnp.int32, (tq, LANES), 1)

    qh = []
    for blk in range(IDX_Q // LANES):
        qb = q_ref[:, blk * LANES:(blk + 1) * LANES]
        qh.append(jnp.where(lane < IDX_DIM, qb, 0.0).astype(BF16))
        qh.append(jnp.where(lane < IDX_DIM, pltpu.roll(qb, IDX_DIM, 1), 0.0).astype(BF16))
    wcol = IDX_Q + IDX_DIM
    wscale = IDX_HEADS ** -0.5 * IDX_DIM ** -0.5
    wts = [q_ref[:, wcol + h:wcol + h + 1] * wscale for h in range(IDX_HEADS)]

    def score_chunk(c, masked):
        k0 = pl.multiple_of(c * IDX_CHUNK, IDX_CHUNK)
        kc = k_ref[pl.ds(k0, IDX_CHUNK), :]
        sc = None
        for h in range(IDX_HEADS):
            r = lax.dot_general(qh[h], kc, (((1,), (1,)), ((), ())), preferred_element_type=F32)
            t = wts[h] * jnp.maximum(r, 0.0)
            sc = t if sc is None else sc + t
        if masked:
            row = q0 + lax.broadcasted_iota(jnp.int32, (tq, IDX_CHUNK), 0)
            col = k0 + lax.broadcasted_iota(jnp.int32, (tq, IDX_CHUNK), 1)
            sc = jnp.where(col <= row, sc, -jnp.inf)
        sc_ref[:, pl.ds(k0, IDX_CHUNK)] = sc

    def unmasked_body(c, carry):
        score_chunk(c, False)
        return carry

    lax.fori_loop(0, nch - 1, unmasked_body, 0)
    score_chunk(nch - 1, True)

    per = COUNT_CHUNK // IDX_CHUNK
    n_cnt = (nch + per - 1) // per

    def pad(c, carry):
        k0 = pl.multiple_of(c * IDX_CHUNK, IDX_CHUNK)
        sc_ref[:, pl.ds(k0, IDX_CHUNK)] = jnp.full((tq, IDX_CHUNK), -jnp.inf, F32)
        return carry
    lax.fori_loop(nch, n_cnt * per, pad, 0)

    def count(pred):
        def body(c, acc):
            k0 = pl.multiple_of(c * COUNT_CHUNK, COUNT_CHUNK)
            for jj in range(COUNT_CHUNK // LANES):
                blk = sc_ref[:, pl.ds(k0 + jj * LANES, LANES)]
                acc = acc + jnp.where(pred(blk), 1.0, 0.0)
            return acc
        acc = lax.fori_loop(0, n_cnt, body, jnp.zeros((tq, LANES), F32))
        return jnp.broadcast_to(jnp.sum(acc, axis=1, keepdims=True), (tq, LANES))

    def bisect(_, carry):
        lo, hi = carry
        mid = (lo >> 1) + (hi >> 1) + (lo & hi & 1)
        cand = _key_to_f32(mid)
        ok = count(lambda blk: blk >= cand) >= float(topk)
        return jnp.where(ok, mid, lo), jnp.where(ok, hi, mid)

    lo0 = jnp.full((tq, LANES), KEY_LO, jnp.int32)
    hi0 = jnp.full((tq, LANES), KEY_HI, jnp.int32)
    lo, _ = lax.fori_loop(0, 32, bisect, (lo0, hi0))
    thr = _key_to_f32(lo)
    n_gt = count(lambda blk: blk > thr)
    n_ge = count(lambda blk: blk >= thr)
    need = float(topk) - n_gt
    has_tie = jnp.max(jnp.where(n_ge > float(topk), 1.0, 0.0)) > 0.0
    thr_c = thr[:, :1]

    @pl.when(jnp.logical_not(has_tie))
    def _():
        def body(c, carry):
            k0 = pl.multiple_of(c * IDX_CHUNK, IDX_CHUNK)
            sc = sc_ref[:, pl.ds(k0, IDX_CHUNK)]
            o_ref[0, :, pl.ds(k0, IDX_CHUNK)] = jnp.where(sc >= thr_c, 0.0, NEG).astype(o_ref.dtype)
            return carry
        lax.fori_loop(0, nch, body, 0)

    @pl.when(has_tie)
    def _():
        r_i = lax.broadcasted_iota(jnp.int32, (IDX_CHUNK, IDX_CHUNK), 0)
        c_i = lax.broadcasted_iota(jnp.int32, (IDX_CHUNK, IDX_CHUNK), 1)
        upper = jnp.where(r_i < c_i, 1.0, 0.0).astype(BF16)
        need_c = need[:, :1]

        def body(c, seen):
            k0 = pl.multiple_of(c * IDX_CHUNK, IDX_CHUNK)
            sc = sc_ref[:, pl.ds(k0, IDX_CHUNK)]
            eq = jnp.where(sc == thr_c, 1.0, 0.0)
            before = seen + jnp.dot(eq.astype(BF16), upper, preferred_element_type=F32)
            keep = (sc > thr_c) | ((sc == thr_c) & (before < need_c))
            o_ref[0, :, pl.ds(k0, IDX_CHUNK)] = jnp.where(keep, 0.0, NEG).astype(o_ref.dtype)
            return seen + jnp.sum(eq, axis=1, keepdims=True)
        lax.fori_loop(0, nch, body, jnp.zeros((tq, 1), F32))

    def fill(c, carry):
        k0 = pl.multiple_of(c * IDX_CHUNK, IDX_CHUNK)
        o_ref[0, :, pl.ds(k0, IDX_CHUNK)] = jnp.full((tq, IDX_CHUNK), NEG, o_ref.dtype)
        return carry
    lax.fori_loop(nch, n_all, fill, 0)


def _indexer(idx, kidx, batch, *, tq=128, name):
    T = idx.shape[0]
    S = T // batch
    tq = min(tq, S)
    ns = S // tq
    topk = min(TOPK_MAX, S // 4)
    assert S % COUNT_CHUNK == 0 and IDX_CHUNK % tq == 0
    return pl.pallas_call(
        functools.partial(_indexer_kernel, tq=tq, topk=topk, seq=S),
        grid=(batch, ns),
        in_specs=[pl.BlockSpec((tq, IDX_PAD), lambda b, i: (b * ns + i, 0)),
                  pl.BlockSpec((S, LANES), lambda b, i: (b, 0))],
        out_specs=pl.BlockSpec((1, tq, S), lambda b, i: (b, i, 0)),
        out_shape=jax.ShapeDtypeStruct((batch, S, S), BF16),
        scratch_shapes=[pltpu.VMEM((tq, S), F32)],
        compiler_params=_cparams(("parallel", "arbitrary")),
        name=name,
    )(idx, kidx)


def _dsa_attn_kernel(qi_ref, kj_ref, q_ref, k_ref, v_ref, bias_ref, o_ref, m_ref, acc_ref, *, tq, tk, rq):
    p_id = pl.program_id(2)
    qi = qi_ref[p_id]
    kj = kj_ref[p_id]
    ratio = tq // tk

    @pl.when(kj == 0)
    def _():
        _flash_init(m_ref, acc_ref)

    def step(d):
        k = k_ref[...]
        v_ext = _with_ones(v_ref[...])
        for rb in range(tq // rq):
            if d is not None and (rb + 1) * rq <= d * tk:
                continue
            rows = slice(rb * rq, (rb + 1) * rq)
            bias = bias_ref[0, rows, :].astype(F32)
            for r in range(DSA_REP):
                q = q_ref[rows, r * DSA_HEAD_DIM:(r + 1) * DSA_HEAD_DIM]
                _flash_update(q, k, v_ext, m_ref, acc_ref, r, rows, bias=bias)

    @pl.when(kj < qi * ratio)
    def _():
        step(None)

    for d in range(ratio):
        @pl.when(kj == qi * ratio + d)
        def _(d=d):
            step(d)

    @pl.when(kj == (qi + 1) * ratio - 1)
    def _():
        for r in range(DSA_REP):
            o = acc_ref[r, :, :DSA_HEAD_DIM] / acc_ref[r, :, DSA_HEAD_DIM:]
            o_ref[:, r * DSA_HEAD_DIM:(r + 1) * DSA_HEAD_DIM] = o.astype(o_ref.dtype)


def _dsa_attn(qkv, bias, batch, *, tq=1024, tk=512, rq=128, name):
    T = qkv.shape[0]
    S = T // batch
    tq, tk = min(tq, S), min(tk, S)
    rq = min(rq, tq)
    assert S % tq == 0 and tq % tk == 0 and tq % rq == 0
    nq, nk = S // tq, S // tk
    qi, kj = _causal_pairs(nq, tq // tk)
    G = DSA_KV_HEADS
    gw = DSA_REP * DSA_HEAD_DIM
    kb = DSA_Q // DSA_HEAD_DIM
    grid_spec = pltpu.PrefetchScalarGridSpec(
        num_scalar_prefetch=2,
        grid=(batch, G, int(qi.shape[0])),
        in_specs=[pl.BlockSpec((tq, gw), lambda b, g, p, qi, kj: (b * nq + qi[p], g)),
                  pl.BlockSpec((tk, DSA_HEAD_DIM), lambda b, g, p, qi, kj: (b * nk + kj[p], kb + g)),
                  pl.BlockSpec((tk, DSA_HEAD_DIM), lambda b, g, p, qi, kj: (b * nk + kj[p], kb + G + g)),
                  pl.BlockSpec((1, tq, tk), lambda b, g, p, qi, kj: (b, qi[p], kj[p]))],
        out_specs=pl.BlockSpec((tq, gw), lambda b, g, p, qi, kj: (b * nq + qi[p], g)),
        scratch_shapes=[pltpu.VMEM((DSA_REP, tq, DSA_HEAD_DIM), F32),
                        pltpu.VMEM((DSA_REP, tq, 2 * DSA_HEAD_DIM), F32)])
    return pl.pallas_call(
        functools.partial(_dsa_attn_kernel, tq=tq, tk=tk, rq=rq),
        grid_spec=grid_spec,
        out_shape=jax.ShapeDtypeStruct((T, DSA_Q), BF16),
        compiler_params=_cparams(("parallel", "parallel", "arbitrary")),
        name=name,
    )(qi, kj, qkv, qkv, qkv, bias)


def _even_mixer(x, positions, batch, w_in, w_out, lam_p, subln_g, conv_w, conv_b, conv_ln_g, conv_ln_b,
                lam_init, ln_g, ln_b):
    tabs, half = _rot_tables(positions, DIFF_HEAD_DIM)
    n_rot = (2 * DIFF_WIDTH) // LANES
    chunk_types = [0] * n_rot + [-1] * ((EVEN_IN - 2 * DIFF_WIDTH) // LANES)
    q_scale = DIFF_HEAD_DIM ** -0.5 * LOG2E
    chunk_scales = [q_scale] * (DIFF_WIDTH // LANES) + [1.0] * ((EVEN_IN - DIFF_WIDTH) // LANES)
    proj = _inproj(x, w_in.astype(BF16), tabs, (half,), chunk_types, BF16, tm=1024, tn=512, name="even_inproj",
                   chunk_scales=chunk_scales)
    lp = lam_p.astype(F32)
    lam = jnp.exp(jnp.sum(lp[0] * lp[1])) - jnp.exp(jnp.sum(lp[2] * lp[3])) + lam_init
    a = _diff_attn(proj, lam, subln_g, batch, lam_init, name="diff_attn")
    c = _conv_module(proj, conv_w, conv_b, conv_ln_g, conv_ln_b, batch, name="conv_module")
    h = jnp.concatenate([a, c], axis=-1)
    return _proj_ln(h, w_out.astype(BF16), x, ln_g, ln_b, name="even_outproj_ln")


def _odd_mixer(x, positions, batch, w_in, w_out, ln_g, ln_b):
    T = x.shape[0]
    tabs_b, half_b = _rot_tables(positions, DSA_HEAD_DIM)
    n_rot = (DSA_Q + DSA_KV) // LANES
    chunk_types = [0] * n_rot + [-1] * (DSA_KV // LANES)
    w_main = w_in[:, :ODD_MAIN].astype(BF16)
    q_scale = DSA_HEAD_DIM ** -0.5 * LOG2E
    chunk_scales = [q_scale] * (DSA_Q // LANES) + [1.0] * ((ODD_MAIN - DSA_Q) // LANES)
    qkv = _inproj(x, w_main, tabs_b, (half_b,), chunk_types, BF16, tm=1024, tn=512, name="odd_inproj",
                  chunk_scales=chunk_scales)
    w_idx = jnp.pad(w_in[:, ODD_MAIN:], ((0, 0), (0, IDX_PAD - (w_in.shape[1] - ODD_MAIN)))).astype(BF16)
    tabs_a, half_a = _rot_tables(positions, IDX_DIM)
    tabs_ah, _ = _rot_tables(positions, IDX_DIM, active_lanes=IDX_DIM)
    idx_types = [0] * (IDX_Q // LANES) + [1] + [-1] * ((IDX_PAD - IDX_Q) // LANES - 1)
    idx = _inproj(x, w_idx, jnp.concatenate([tabs_a, tabs_ah]), (half_a, half_a), idx_types, F32,
                  tm=1024, tn=IDX_PAD, name="idx_inproj")
    kidx = idx[:, IDX_Q:IDX_Q + LANES].astype(BF16)
    bias = _indexer(idx, kidx, batch, name="indexer")
    o = _dsa_attn(qkv, bias, batch, name="dsa_attn")
    return _proj_ln(o, w_out.astype(BF16), x, ln_g, ln_b, name="odd_outproj_ln")


def kernel(x, mem, positions, w_in_even, w_out_even, diff_lambda, diff_subln_g, conv_w, conv_b, conv_ln_g,
           conv_ln_b, w_in_odd, w_out_odd, xa_wq, xa_wkv, xa_wo, ffn_w_in, ffn_w_out, ln_g, ln_b):
    B, S, D = x.shape
    M = mem.shape[1]
    x = x.reshape(B * S, D)
    mem2 = mem.reshape(B * M, D)
    for layer in range(DEPTH):
        j = layer // 2
        if layer % 2 == 0:
            lam_init = 0.8 - 0.6 * math.exp(-0.3 * layer)
            x = _even_mixer(x, positions, B, w_in_even[j], w_out_even[j], diff_lambda[j], diff_subln_g[j],
                            conv_w[j], conv_b[j], conv_ln_g[j], conv_ln_b[j], lam_init,
                            ln_g[layer, 0], ln_b[layer, 0])
        else:
            x = _odd_mixer(x, positions, B, w_in_odd[j], w_out_odd[j], ln_g[layer, 0], ln_b[layer, 0])
        kv = _inproj(mem2, xa_wkv[layer].astype(BF16), None, (), [-1] * (2 * D // LANES), BF16,
                     tm=512, tn=512, name=f"xa_kvproj_{layer}")
        o = _xattn(x, xa_wq[layer].astype(BF16), kv, B, name=f"xattn_{layer}")
        x = _proj_ln(o, xa_wo[layer].astype(BF16), x, ln_g[layer, 1], ln_b[layer, 1], name=f"xa_outproj_ln_{layer}")
        x = _ffn(x, ffn_w_in[layer].astype(BF16), ffn_w_out[layer].astype(BF16),
                 ln_g[layer, 2], ln_b[layer, 2], name=f"ffn_{layer}")
    return x.reshape(B, S, D)
```

```python
import functools
import math

import jax
import jax.numpy as jnp
from jax import lax
from jax.experimental import pallas as pl
from jax.experimental.pallas import tpu as pltpu

F32 = jnp.float32
BF16 = jnp.bfloat16

D_MODEL = 2048
DEPTH = 2
ALPHA = (2 * DEPTH) ** 0.25
LN_EPS = 1e-5
ROPE_THETA = 500000.0
ROPE_FRAC = 4

DIFF_HEADS = 8
DIFF_HEAD_DIM = 64
DIFF_V_DIM = 2 * DIFF_HEAD_DIM
DIFF_WIDTH = DIFF_HEADS * DIFF_V_DIM
CONV_CH = D_MODEL - DIFF_WIDTH
CONV_WIDTH = 31
EVEN_IN = 3 * DIFF_WIDTH + 2 * CONV_CH

DSA_HEADS = 16
DSA_KV_HEADS = 4
DSA_HEAD_DIM = 128
DSA_REP = DSA_HEADS // DSA_KV_HEADS
IDX_HEADS = 4
IDX_DIM = 64
TOPK_MAX = 256
DSA_Q = DSA_HEADS * DSA_HEAD_DIM
DSA_KV = DSA_KV_HEADS * DSA_HEAD_DIM
IDX_Q = IDX_HEADS * IDX_DIM
ODD_MAIN = DSA_Q + 2 * DSA_KV
IDX_PAD = 512

XA_HEADS = 4
XA_HEAD_DIM = D_MODEL // XA_HEADS
D_FF = -(-(8 * D_MODEL) // (3 * 256)) * 256

LANES = 128
NEG = -1e30
VMEM_LIMIT = 56 * 1024 * 1024


def _cparams(sem):
    return pltpu.CompilerParams(dimension_semantics=sem, vmem_limit_bytes=VMEM_LIMIT)


def _layer_norm_rows(y, g, b):
    mu = jnp.mean(y, axis=-1, keepdims=True)
    d = y - mu
    var = jnp.mean(d * d, axis=-1, keepdims=True)
    return d * lax.rsqrt(var + LN_EPS) * g + b


def _rot_tables(positions, head_dim, active_lanes=LANES):
    rot = head_dim // ROPE_FRAC
    half = rot // 2
    inv_freq = 1.0 / (ROPE_THETA ** (jnp.arange(half, dtype=F32) / half))
    ang = positions.reshape(-1, 1).astype(F32) * inv_freq
    cos, sin = jnp.cos(ang), jnp.sin(ang)
    lane = jnp.arange(LANES)
    d = lane % head_dim
    f = d % half
    live = lane < active_lanes
    cos_l = jnp.take(cos, f, axis=1)
    sin_l = jnp.take(sin, f, axis=1)
    c = jnp.where((d < rot) & live, cos_l, 1.0)
    s1 = jnp.where((d >= half) & (d < rot) & live, sin_l, 0.0)
    s2 = jnp.where((d < half) & live, -sin_l, 0.0)
    return jnp.stack([c, s1, s2]), half


def _inproj_kernel(x_ref, w_ref, tab_ref, o_ref, xb_ref, *, variants, shifts):
    j = pl.program_id(1)

    @pl.when(j == 0)
    def _():
        xb_ref[...] = x_ref[...].astype(BF16)

    acc = jnp.dot(xb_ref[...], w_ref[...], preferred_element_type=F32)

    for (j_lo, j_hi), types in variants:
        @pl.when((j >= j_lo) & (j < j_hi))
        def _(types=types):
            for c, (p, scale) in enumerate(types):
                blk = acc[:, c * LANES:(c + 1) * LANES]
                if p >= 0:
                    sh = shifts[p]
                    blk = (blk * tab_ref[3 * p]
                           + pltpu.roll(blk, sh, 1) * tab_ref[3 * p + 1]
                           + pltpu.roll(blk, LANES - sh, 1) * tab_ref[3 * p + 2])
                if scale != 1.0:
                    blk = blk * scale
                o_ref[:, c * LANES:(c + 1) * LANES] = blk.astype(o_ref.dtype)


def _inproj(x, w, tabs, shifts, chunk_types, out_dtype, *, tm, tn, name, chunk_scales=None):
    T, K = x.shape
    N = w.shape[1]
    tm = min(tm, T)
    assert T % tm == 0 and N % tn == 0 and len(chunk_types) == N // LANES
    if chunk_scales is None:
        chunk_scales = [1.0] * len(chunk_types)
    chunk_types = list(zip(chunk_types, chunk_scales))
    per = tn // LANES
    tile_types = [tuple(chunk_types[t * per:(t + 1) * per]) for t in range(N // tn)]
    variants = []
    for t, ty in enumerate(tile_types):
        if variants and variants[-1][1] == ty and variants[-1][0][1] == t:
            variants[-1] = ((variants[-1][0][0], t + 1), ty)
        else:
            variants.append(((t, t + 1), ty))
    if tabs is None:
        tabs = jnp.zeros((3, T, LANES), F32)
    P3 = tabs.shape[0]
    return pl.pallas_call(
        functools.partial(_inproj_kernel, variants=tuple(variants), shifts=tuple(shifts)),
        grid=(T // tm, N // tn),
        in_specs=[pl.BlockSpec((tm, K), lambda i, j: (i, 0)),
                  pl.BlockSpec((K, tn), lambda i, j: (0, j)),
                  pl.BlockSpec((P3, tm, LANES), lambda i, j: (0, i, 0))],
        out_specs=pl.BlockSpec((tm, tn), lambda i, j: (i, j)),
        out_shape=jax.ShapeDtypeStruct((T, N), out_dtype),
        scratch_shapes=[pltpu.VMEM((tm, K), BF16)],
        compiler_params=_cparams(("parallel", "arbitrary")),
        name=name,
    )(x, w, tabs)


def _proj_ln_kernel(h_ref, w_ref, x_ref, g_ref, b_ref, o_ref):
    y = ALPHA * x_ref[...] + jnp.dot(h_ref[...], w_ref[...], preferred_element_type=F32)
    o_ref[...] = _layer_norm_rows(y, g_ref[...], b_ref[...])


def _proj_ln(h, w, x, g, b, *, tm=512, name):
    T, K = h.shape
    D = w.shape[1]
    tm = min(tm, T)
    assert T % tm == 0
    return pl.pallas_call(
        _proj_ln_kernel,
        grid=(T // tm,),
        in_specs=[pl.BlockSpec((tm, K), lambda i: (i, 0)),
                  pl.BlockSpec((K, D), lambda i: (0, 0), pipeline_mode=pl.Buffered(1)),
                  pl.BlockSpec((tm, D), lambda i: (i, 0)),
                  pl.BlockSpec((1, D), lambda i: (0, 0)),
                  pl.BlockSpec((1, D), lambda i: (0, 0))],
        out_specs=pl.BlockSpec((tm, D), lambda i: (i, 0)),
        out_shape=jax.ShapeDtypeStruct((T, D), F32),
        compiler_params=_cparams(("parallel",)),
        name=name,
    )(h, w, x, g.reshape(1, D), b.reshape(1, D))


def _ffn_kernel(x_ref, wg_ref, wu_ref, wo_ref, g_ref, b_ref, o_ref, xb_ref, acc_ref):
    f = pl.program_id(1)

    @pl.when(f == 0)
    def _():
        xb_ref[...] = x_ref[...].astype(BF16)
        acc_ref[...] = jnp.zeros_like(acc_ref)

    xb = xb_ref[...]
    gate = jnp.dot(xb, wg_ref[...], preferred_element_type=F32)
    up = jnp.dot(xb, wu_ref[...], preferred_element_type=F32)
    h = (gate * jax.nn.sigmoid(gate) * up).astype(BF16)
    acc_ref[...] += jnp.dot(h, wo_ref[...], preferred_element_type=F32)

    @pl.when(f == pl.num_programs(1) - 1)
    def _():
        y = ALPHA * x_ref[...] + acc_ref[...]
        o_ref[...] = _layer_norm_rows(y, g_ref[...], b_ref[...])


def _ffn(x, w_in, w_out, g, b, *, tm=512, tf=512, name):
    T, D = x.shape
    dff = w_out.shape[0]
    tm = min(tm, T)
    assert T % tm == 0 and dff % tf == 0
    nf = dff // tf
    return pl.pallas_call(
        _ffn_kernel,
        grid=(T // tm, nf),
        in_specs=[pl.BlockSpec((tm, D), lambda i, f: (i, 0)),
                  pl.BlockSpec((D, tf), lambda i, f: (0, f)),
                  pl.BlockSpec((D, tf), lambda i, f: (0, f + nf)),
                  pl.BlockSpec((tf, D), lambda i, f: (f, 0)),
                  pl.BlockSpec((1, D), lambda i, f: (0, 0)),
                  pl.BlockSpec((1, D), lambda i, f: (0, 0))],
        out_specs=pl.BlockSpec((tm, D), lambda i, f: (i, 0)),
        out_shape=jax.ShapeDtypeStruct((T, D), F32),
        scratch_shapes=[pltpu.VMEM((tm, D), BF16), pltpu.VMEM((tm, D), F32)],
        compiler_params=_cparams(("parallel", "arbitrary")),
        name=name,
    )(x, w_in, w_in, w_out, g.reshape(1, D), b.reshape(1, D))


def _xattn_kernel(x_ref, wq_ref, k_ref, v_ref, o_ref):
    xb = x_ref[...].astype(BF16)
    q = jnp.dot(xb, wq_ref[...], preferred_element_type=F32).astype(BF16)
    scale = XA_HEAD_DIM ** -0.5
    for h in range(XA_HEADS):
        sl = slice(h * XA_HEAD_DIM, (h + 1) * XA_HEAD_DIM)
        s = lax.dot_general(q[:, sl], k_ref[:, sl], (((1,), (1,)), ((), ())),
                            preferred_element_type=F32) * scale
        m = jnp.max(s, axis=-1, keepdims=True)
        p = jnp.exp(s - m)
        l = jnp.sum(p, axis=-1, keepdims=True)
        p = (p / l).astype(BF16)
        o = jnp.dot(p, v_ref[:, sl], preferred_element_type=F32)
        o_ref[:, sl] = o.astype(o_ref.dtype)


def _xattn(x, wq, kv, batch, *, tm=512, name):
    T, D = x.shape
    S = T // batch
    M = kv.shape[0] // batch
    tm = min(tm, S)
    ns = S // tm
    return pl.pallas_call(
        _xattn_kernel,
        grid=(batch, ns),
        in_specs=[pl.BlockSpec((tm, D), lambda b, i: (b * ns + i, 0)),
                  pl.BlockSpec((D, D), lambda b, i: (0, 0)),
                  pl.BlockSpec((M, D), lambda b, i: (b, 0)),
                  pl.BlockSpec((M, D), lambda b, i: (b, 1))],
        out_specs=pl.BlockSpec((tm, D), lambda b, i: (b * ns + i, 0)),
        out_shape=jax.ShapeDtypeStruct((T, D), BF16),
        compiler_params=_cparams(("parallel", "arbitrary")),
        name=name,
    )(x, wq, kv, kv)


NT_DIMS = (((1,), (1,)), ((), ()))
LOG2E = math.log2(math.e)


def _causal_pairs(nq, ratio):
    qi = [i for i in range(nq) for _ in range((i + 1) * ratio)]
    kj = [j for i in range(nq) for j in range((i + 1) * ratio)]
    return jnp.asarray(qi, jnp.int32), jnp.asarray(kj, jnp.int32)


def _flash_update(q, k, v_ext, m_ref, acc_ref, idx, rows, bias=None, mask=None):
    s = lax.dot_general(q, k, NT_DIMS, preferred_element_type=F32)
    if bias is not None:
        s = s + bias
    if mask is not None:
        s = jnp.where(mask, s, NEG)
    chunks = [s[:, j * LANES:(j + 1) * LANES] for j in range(s.shape[1] // LANES)]
    mc = chunks[0]
    for ch in chunks[1:]:
        mc = jnp.maximum(mc, ch)
    m_prev = m_ref[idx, rows, :]
    m_new = jnp.maximum(m_prev, jnp.max(mc, axis=1, keepdims=True))
    p = jnp.concatenate([jnp.exp2(ch - m_new).astype(BF16) for ch in chunks], axis=1)
    alpha = jnp.exp2(m_prev - m_new)
    pv = jnp.dot(p, v_ext, preferred_element_type=F32)
    acc_ref[idx, rows, :] = jnp.concatenate([alpha, alpha], axis=1) * acc_ref[idx, rows, :] + pv
    m_ref[idx, rows, :] = m_new


def _flash_init(m_ref, acc_ref):
    m_ref[...] = jnp.full_like(m_ref, -jnp.inf)
    acc_ref[...] = jnp.zeros_like(acc_ref)


def _with_ones(v):
    return jnp.concatenate([v, jnp.ones_like(v)], axis=1)


def _diff_attn_kernel(lam_ref, q_ref, k_ref, v_ref, g_ref, o_ref, m_ref, acc_ref, *, tq, tk, rq, lam_init):
    qi = pl.program_id(2)
    ratio = tq // tk
    _flash_init(m_ref, acc_ref)

    def step(kj, d):
        k0 = pl.multiple_of(kj * tk, tk)
        k = k_ref[pl.ds(k0, tk), :]
        v_ext = _with_ones(v_ref[pl.ds(k0, tk), :])
        lane = lax.broadcasted_iota(jnp.int32, (rq, LANES), 1)
        for rb in range(tq // rq):
            if d is not None and (rb + 1) * rq <= d * tk:
                continue
            rows = slice(rb * rq, (rb + 1) * rq)
            q = q_ref[rows, :]
            mask = None
            if d is not None and rb * rq < (d + 1) * tk - 1:
                row = rb * rq + lax.broadcasted_iota(jnp.int32, (rq, tk), 0)
                col = d * tk + lax.broadcasted_iota(jnp.int32, (rq, tk), 1)
                mask = col <= row
            for c, sel in enumerate((lane < DIFF_HEAD_DIM, lane >= DIFF_HEAD_DIM)):
                _flash_update(jnp.where(sel, q, jnp.zeros_like(q)), k, v_ext, m_ref, acc_ref, c, rows, mask=mask)

    def below_diagonal(kj, carry):
        step(kj, None)
        return carry

    lax.fori_loop(0, qi * ratio, below_diagonal, 0)
    for d in range(ratio):
        step(qi * ratio + d, d)

    lam = lam_ref[0]
    a = (acc_ref[0, :, :LANES] / acc_ref[0, :, LANES:]
         - lam * (acc_ref[1, :, :LANES] / acc_ref[1, :, LANES:]))
    ms = jnp.mean(a * a, axis=-1, keepdims=True)
    a = a * lax.rsqrt(ms + LN_EPS) * g_ref[...]
    o_ref[...] = (a * (1.0 - lam_init)).astype(o_ref.dtype)


def _diff_attn(proj, lam, subln_g, batch, lam_init, *, tq=1024, tk=1024, rq=128, name):
    T = proj.shape[0]
    S = T // batch
    tq, tk = min(tq, S), min(tk, S)
    rq = min(rq, tq)
    assert S % tq == 0 and tq % tk == 0 and tq % rq == 0
    nq = S // tq
    H = DIFF_HEADS
    return pl.pallas_call(
        functools.partial(_diff_attn_kernel, tq=tq, tk=tk, rq=rq, lam_init=lam_init),
        grid=(batch, H, nq),
        in_specs=[pl.BlockSpec(memory_space=pltpu.SMEM),
                  pl.BlockSpec((tq, LANES), lambda b, h, i: (b * nq + i, h)),
                  pl.BlockSpec((S, LANES), lambda b, h, i: (b, H + h)),
                  pl.BlockSpec((S, LANES), lambda b, h, i: (b, 2 * H + h)),
                  pl.BlockSpec((1, LANES), lambda b, h, i: (0, 0))],
        out_specs=pl.BlockSpec((tq, LANES), lambda b, h, i: (b * nq + i, h)),
        out_shape=jax.ShapeDtypeStruct((T, DIFF_WIDTH), BF16),
        scratch_shapes=[pltpu.VMEM((2, tq, LANES), F32), pltpu.VMEM((2, tq, 2 * LANES), F32)],
        compiler_params=_cparams(("parallel", "parallel", "arbitrary")),
        name=name,
    )(lam.reshape(1).astype(F32), proj, proj, proj, subln_g.reshape(1, LANES).astype(F32))


CONV_HALO = 32
CONV_ROWS = 64


def _conv_kernel(val_ref, gate_ref, hval_ref, hgate_ref, cw_ref, cb_ref, g_ref, b_ref, o_ref,
                 u_ref, c_ref, *, tr):
    i = pl.program_id(1)
    u_ref[CONV_HALO:, :] = val_ref[...].astype(F32) * jax.nn.sigmoid(gate_ref[...].astype(F32))
    halo = hval_ref[...].astype(F32) * jax.nn.sigmoid(hgate_ref[...].astype(F32))
    u_ref[:CONV_HALO, :] = jnp.where(i > 0, halo, 0.0)
    base = CONV_HALO - (CONV_WIDTH - 1)
    for r in range(tr // CONV_ROWS):
        for c in range(CONV_CH // LANES):
            cs = slice(c * LANES, (c + 1) * LANES)
            acc = jnp.broadcast_to(cb_ref[:, cs], (CONV_ROWS, LANES))
            for w in range(CONV_WIDTH):
                r0 = r * CONV_ROWS + base + w
                acc = acc + u_ref[r0:r0 + CONV_ROWS, cs] * cw_ref[w:w + 1, cs]
            c_ref[r * CONV_ROWS:(r + 1) * CONV_ROWS, cs] = acc
    y = _layer_norm_rows(c_ref[...], g_ref[...], b_ref[...])
    o_ref[...] = (y * jax.nn.sigmoid(y)).astype(o_ref.dtype)


def _conv_module(proj, conv_w, conv_b, ln_g, ln_b, batch, *, tr=256, name):
    T = proj.shape[0]
    S = T // batch
    tr = min(tr, S)
    ns = S // tr
    vb = (3 * DIFF_WIDTH) // CONV_CH
    hpb = tr // CONV_HALO

    def halo_idx(col):
        return lambda b, i: (jnp.maximum((b * ns + i) * hpb - 1, 0), col)

    vec = lambda a: a.reshape(1, CONV_CH).astype(F32)
    return pl.pallas_call(
        functools.partial(_conv_kernel, tr=tr),
        grid=(batch, ns),
        in_specs=[pl.BlockSpec((tr, CONV_CH), lambda b, i: (b * ns + i, vb)),
                  pl.BlockSpec((tr, CONV_CH), lambda b, i: (b * ns + i, vb + 1)),
                  pl.BlockSpec((CONV_HALO, CONV_CH), halo_idx(vb)),
                  pl.BlockSpec((CONV_HALO, CONV_CH), halo_idx(vb + 1)),
                  pl.BlockSpec((CONV_WIDTH, CONV_CH), lambda b, i: (0, 0)),
                  pl.BlockSpec((1, CONV_CH), lambda b, i: (0, 0)),
                  pl.BlockSpec((1, CONV_CH), lambda b, i: (0, 0)),
                  pl.BlockSpec((1, CONV_CH), lambda b, i: (0, 0))],
        out_specs=pl.BlockSpec((tr, CONV_CH), lambda b, i: (b * ns + i, 0)),
        out_shape=jax.ShapeDtypeStruct((T, CONV_CH), BF16),
        scratch_shapes=[pltpu.VMEM((CONV_HALO + tr, CONV_CH), F32), pltpu.VMEM((tr, CONV_CH), F32)],
        compiler_params=_cparams(("parallel", "arbitrary")),
        name=name,
    )(proj, proj, proj, proj, conv_w.astype(F32), vec(conv_b), vec(ln_g), vec(ln_b))


IDX_CHUNK = 512
COUNT_CHUNK = 1024
KEY16_LO = -32640
KEY16_HI = 0x7F80
KEY16_BITS = 16


def _key_to_f32(key):
    bits = jnp.where(key < 0, key ^ jnp.int32(0x7FFFFFFF), key)
    return pltpu.bitcast(bits, F32)


def _indexer_kernel(q_ref, k_ref, o_ref, sc_ref, g_ref, *, tq, topk, seq):
    i = pl.program_id(1)
    q0 = i * tq
    nch = (q0 + tq + IDX_CHUNK - 1) // IDX_CHUNK
    n_all = seq // IDX_CHUNK
    lane = lax.broadcasted_iota(jnp.int32, (tq, LANES), 1)

    qh = []
    for blk in range(IDX_Q // LANES):
        qb = q_ref[:, blk * LANES:(blk + 1) * LANES]
        qh.append(jnp.where(lane < IDX_DIM, qb, 0.0).astype(BF16))
        qh.append(jnp.where(lane < IDX_DIM, pltpu.roll(qb, IDX_DIM, 1), 0.0).astype(BF16))
    wcol = IDX_Q + IDX_DIM
    wscale = IDX_HEADS ** -0.5 * IDX_DIM ** -0.5
    wts = [q_ref[:, wcol + h:wcol + h + 1] * wscale for h in range(IDX_HEADS)]

    def score_chunk(c, masked):
        k0 = pl.multiple_of(c * IDX_CHUNK, IDX_CHUNK)
        kc = k_ref[pl.ds(k0, IDX_CHUNK), :]
        sc = None
        for h in range(IDX_HEADS):
            r = lax.dot_general(qh[h], kc, (((1,), (1,)), ((), ())), preferred_element_type=F32)
            t = wts[h] * jnp.maximum(r, 0.0)
            sc = t if sc is None else sc + t
        if masked:
            row = q0 + lax.broadcasted_iota(jnp.int32, (tq, IDX_CHUNK), 0)
            col = k0 + lax.broadcasted_iota(jnp.int32, (tq, IDX_CHUNK), 1)
            sc = jnp.where(col <= row, sc, -jnp.inf)
        sc_ref[:, pl.ds(k0, IDX_CHUNK)] = sc
        hi_bits = pltpu.bitcast(sc, jnp.int32) & jnp.int32(-(1 << KEY16_BITS))
        g_ref[:, pl.ds(k0, IDX_CHUNK)] = pltpu.bitcast(hi_bits, F32).astype(BF16)

    def unmasked_body(c, carry):
        score_chunk(c, False)
        return carry

    lax.fori_loop(0, nch - 1, unmasked_body, 0)
    score_chunk(nch - 1, True)

    per = COUNT_CHUNK // IDX_CHUNK
    n_cnt = (nch + per - 1) // per

    def pad(c, carry):
        k0 = pl.multiple_of(c * IDX_CHUNK, IDX_CHUNK)
        sc_ref[:, pl.ds(k0, IDX_CHUNK)] = jnp.full((tq, IDX_CHUNK), -jnp.inf, F32)
        g_ref[:, pl.ds(k0, IDX_CHUNK)] = jnp.full((tq, IDX_CHUNK), -jnp.inf, BF16)
        return carry
    lax.fori_loop(nch, n_cnt * per, pad, 0)

    def count_in(ref, dtype, pred):
        one = jnp.ones((tq, LANES), dtype)
        zero = jnp.zeros((tq, LANES), dtype)

        def body(c, acc):
            k0 = pl.multiple_of(c * COUNT_CHUNK, COUNT_CHUNK)
            for jj in range(COUNT_CHUNK // LANES):
                blk = ref[:, pl.ds(k0 + jj * LANES, LANES)]
                acc = acc + jnp.where(pred(blk), one, zero)
            return acc
        acc = lax.fori_loop(0, n_cnt, body, zero)
        return jnp.broadcast_to(jnp.sum(acc.astype(F32), axis=1, keepdims=True), (tq, LANES))

    def count(pred):
        return count_in(sc_ref, F32, pred)

    kf = float(topk)
    def bisect16(_, carry):
        lo, hi, cnt_lo = carry
        mid = (lo + hi) >> 1
        key = jnp.where(mid < 0, (mid << KEY16_BITS) | jnp.int32(0xFFFF), mid << KEY16_BITS)
        cand = _key_to_f32(key).astype(BF16)
        cnt = count_in(g_ref, BF16, lambda blk: blk >= cand)
        ok = cnt >= kf
        return jnp.where(ok, mid, lo), jnp.where(ok, hi, mid), jnp.where(ok, cnt, cnt_lo)

    lo16 = jnp.full((tq, LANES), KEY16_LO, jnp.int32)
    hi16 = jnp.full((tq, LANES), KEY16_HI, jnp.int32)
    lo16, _, cnt_lo = lax.fori_loop(0, KEY16_BITS, bisect16, (lo16, hi16, jnp.zeros((tq, LANES), F32)))

    def unresolved(carry):
        lo, hi, _ = carry
        return jnp.max(hi - lo) > 1

    def bisect32(carry):
        lo, hi, cnt_lo = carry
        mid = lo + ((hi - lo) >> 1)
        cand = _key_to_f32(mid)
        cnt = count(lambda blk: blk >= cand)
        ok = cnt >= kf
        hi = jnp.where(cnt == kf, mid + 1, jnp.where(ok, hi, mid))
        return jnp.where(ok, mid, lo), hi, jnp.where(ok, cnt, cnt_lo)

    lo, _, n_ge = lax.while_loop(unresolved, bisect32,
                                 (lo16 << KEY16_BITS, (lo16 + 1) << KEY16_BITS, cnt_lo))
    thr = _key_to_f32(lo)
    has_tie = jnp.max(n_ge) > kf
    thr_c = thr[:, :1]

    @pl.when(jnp.logical_not(has_tie))
    def _():
        def body(c, carry):
            k0 = pl.multiple_of(c * IDX_CHUNK, IDX_CHUNK)
            sc = sc_ref[:, pl.ds(k0, IDX_CHUNK)]
            o_ref[0, :, pl.ds(k0, IDX_CHUNK)] = jnp.where(sc >= thr_c, 0.0, NEG).astype(o_ref.dtype)
            return carry
        lax.fori_loop(0, nch, body, 0)

    @pl.when(has_tie)
    def _():
        r_i = lax.broadcasted_iota(jnp.int32, (IDX_CHUNK, IDX_CHUNK), 0)
        c_i = lax.broadcasted_iota(jnp.int32, (IDX_CHUNK, IDX_CHUNK), 1)
        upper = jnp.where(r_i < c_i, 1.0, 0.0).astype(BF16)
        need_c = (kf - count(lambda blk: blk > thr))[:, :1]

        def body(c, seen):
            k0 = pl.multiple_of(c * IDX_CHUNK, IDX_CHUNK)
            sc = sc_ref[:, pl.ds(k0, IDX_CHUNK)]
            eq = jnp.where(sc == thr_c, 1.0, 0.0)
            before = seen + jnp.dot(eq.astype(BF16), upper, preferred_element_type=F32)
            keep = (sc > thr_c) | ((sc == thr_c) & (before < need_c))
            o_ref[0, :, pl.ds(k0, IDX_CHUNK)] = jnp.where(keep, 0.0, NEG).astype(o_ref.dtype)
            return seen + jnp.sum(eq, axis=1, keepdims=True)
        lax.fori_loop(0, nch, body, jnp.zeros((tq, 1), F32))

    def fill(c, carry):
        k0 = pl.multiple_of(c * IDX_CHUNK, IDX_CHUNK)
        o_ref[0, :, pl.ds(k0, IDX_CHUNK)] = jnp.full((tq, IDX_CHUNK), NEG, o_ref.dtype)
        return carry
    lax.fori_loop(nch, n_all, fill, 0)


def _indexer(idx, kidx, batch, *, tq=128, name):
    T = idx.shape[0]
    S = T // batch
    tq = min(tq, S)
    ns = S // tq
    topk = min(TOPK_MAX, S // 4)
    assert S % COUNT_CHUNK == 0 and IDX_CHUNK % tq == 0
    return pl.pallas_call(
        functools.partial(_indexer_kernel, tq=tq, topk=topk, seq=S),
        grid=(batch, ns),
        in_specs=[pl.BlockSpec((tq, IDX_PAD), lambda b, i: (b * ns + i, 0)),
                  pl.BlockSpec((S, LANES), lambda b, i: (b, 0))],
        out_specs=pl.BlockSpec((1, tq, S), lambda b, i: (b, i, 0)),
        out_shape=jax.ShapeDtypeStruct((batch, S, S), BF16),
        scratch_shapes=[pltpu.VMEM((tq, S), F32), pltpu.VMEM((tq, S), BF16)],
        compiler_params=_cparams(("parallel", "arbitrary")),
        name=name,
    )(idx, kidx)


def _dsa_attn_kernel(qi_ref, kj_ref, q_ref, k_ref, v_ref, bias_ref, o_ref, m_ref, acc_ref, *, tq, tk, rq):
    p_id = pl.program_id(2)
    qi = qi_ref[p_id]
    kj = kj_ref[p_id]
    ratio = tq // tk

    @pl.when(kj == 0)
    def _():
        _flash_init(m_ref, acc_ref)

    def step(d):
        k = k_ref[...]
        v_ext = _with_ones(v_ref[...])
        for rb in range(tq // rq):
            if d is not None and (rb + 1) * rq <= d * tk:
                continue
            rows = slice(rb * rq, (rb + 1) * rq)
            bias = bias_ref[0, rows, :].astype(F32)
            for r in range(DSA_REP):
                q = q_ref[rows, r * DSA_HEAD_DIM:(r + 1) * DSA_HEAD_DIM]
                _flash_update(q, k, v_ext, m_ref, acc_ref, r, rows, bias=bias)

    @pl.when(kj < qi * ratio)
    def _():
        step(None)

    for d in range(ratio):
        @pl.when(kj == qi * ratio + d)
        def _(d=d):
            step(d)

    @pl.when(kj == (qi + 1) * ratio - 1)
    def _():
        for r in range(DSA_REP):
            o = acc_ref[r, :, :DSA_HEAD_DIM] / acc_ref[r, :, DSA_HEAD_DIM:]
            o_ref[:, r * DSA_HEAD_DIM:(r + 1) * DSA_HEAD_DIM] = o.astype(o_ref.dtype)


def _dsa_attn(qkv, bias, batch, *, tq=1024, tk=1024, rq=128, name):
    T = qkv.shape[0]
    S = T // batch
    tq, tk = min(tq, S), min(tk, S)
    rq = min(rq, tq)
    assert S % tq == 0 and tq % tk == 0 and tq % rq == 0
    nq, nk = S // tq, S // tk
    qi, kj = _causal_pairs(nq, tq // tk)
    G = DSA_KV_HEADS
    gw = DSA_REP * DSA_HEAD_DIM
    kb = DSA_Q // DSA_HEAD_DIM
    grid_spec = pltpu.PrefetchScalarGridSpec(
        num_scalar_prefetch=2,
        grid=(batch, G, int(qi.shape[0])),
        in_specs=[pl.BlockSpec((tq, gw), lambda b, g, p, qi, kj: (b * nq + qi[p], g)),
                  pl.BlockSpec((tk, DSA_HEAD_DIM), lambda b, g, p, qi, kj: (b * nk + kj[p], kb + g)),
                  pl.BlockSpec((tk, DSA_HEAD_DIM), lambda b, g, p, qi, kj: (b * nk + kj[p], kb + G + g)),
                  pl.BlockSpec((1, tq, tk), lambda b, g, p, qi, kj: (b, qi[p], kj[p]))],
        out_specs=pl.BlockSpec((tq, gw), lambda b, g, p, qi, kj: (b * nq + qi[p], g)),
        scratch_shapes=[pltpu.VMEM((DSA_REP, tq, DSA_HEAD_DIM), F32),
                        pltpu.VMEM((DSA_REP, tq, 2 * DSA_HEAD_DIM), F32)])
    return pl.pallas_call(
        functools.partial(_dsa_attn_kernel, tq=tq, tk=tk, rq=rq),
        grid_spec=grid_spec,
        out_shape=jax.ShapeDtypeStruct((T, DSA_Q), BF16),
        compiler_params=_cparams(("parallel", "parallel", "arbitrary")),
        name=name,
    )(qi, kj, qkv, qkv, qkv, bias)


def _even_mixer(x, positions, batch, w_in, w_out, lam_p, subln_g, conv_w, conv_b, conv_ln_g, conv_ln_b,
                lam_init, ln_g, ln_b):
    tabs, half = _rot_tables(positions, DIFF_HEAD_DIM)
    n_rot = (2 * DIFF_WIDTH) // LANES
    chunk_types = [0] * n_rot + [-1] * ((EVEN_IN - 2 * DIFF_WIDTH) // LANES)
    q_scale = DIFF_HEAD_DIM ** -0.5 * LOG2E
    chunk_scales = [q_scale] * (DIFF_WIDTH // LANES) + [1.0] * ((EVEN_IN - DIFF_WIDTH) // LANES)
    proj = _inproj(x, w_in.astype(BF16), tabs, (half,), chunk_types, BF16, tm=1024, tn=1024, name="even_inproj",
                   chunk_scales=chunk_scales)
    lp = lam_p.astype(F32)
    lam = jnp.exp(jnp.sum(lp[0] * lp[1])) - jnp.exp(jnp.sum(lp[2] * lp[3])) + lam_init
    a = _diff_attn(proj, lam, subln_g, batch, lam_init, name="diff_attn")
    c = _conv_module(proj, conv_w, conv_b, conv_ln_g, conv_ln_b, batch, name="conv_module")
    h = jnp.concatenate([a, c], axis=-1)
    return _proj_ln(h, w_out.astype(BF16), x, ln_g, ln_b, name="even_outproj_ln")


def _odd_mixer(x, positions, batch, w_in, w_out, ln_g, ln_b):
    T = x.shape[0]
    tabs_b, half_b = _rot_tables(positions, DSA_HEAD_DIM)
    n_rot = (DSA_Q + DSA_KV) // LANES
    chunk_types = [0] * n_rot + [-1] * (DSA_KV // LANES)
    w_main = w_in[:, :ODD_MAIN].astype(BF16)
    q_scale = DSA_HEAD_DIM ** -0.5 * LOG2E
    chunk_scales = [q_scale] * (DSA_Q // LANES) + [1.0] * ((ODD_MAIN - DSA_Q) // LANES)
    qkv = _inproj(x, w_main, tabs_b, (half_b,), chunk_types, BF16, tm=1024, tn=1024, name="odd_inproj",
                  chunk_scales=chunk_scales)
    w_idx = jnp.pad(w_in[:, ODD_MAIN:], ((0, 0), (0, IDX_PAD - (w_in.shape[1] - ODD_MAIN)))).astype(BF16)
    tabs_a, half_a = _rot_tables(positions, IDX_DIM)
    tabs_ah, _ = _rot_tables(positions, IDX_DIM, active_lanes=IDX_DIM)
    idx_types = [0] * (IDX_Q // LANES) + [1] + [-1] * ((IDX_PAD - IDX_Q) // LANES - 1)
    idx = _inproj(x, w_idx, jnp.concatenate([tabs_a, tabs_ah]), (half_a, half_a), idx_types, F32,
                  tm=1024, tn=IDX_PAD, name="idx_inproj")
    kidx = idx[:, IDX_Q:IDX_Q + LANES].astype(BF16)
    bias = _indexer(idx, kidx, batch, name="indexer")
    o = _dsa_attn(qkv, bias, batch, name="dsa_attn")
    return _proj_ln(o, w_out.astype(BF16), x, ln_g, ln_b, name="odd_outproj_ln")


def kernel(x, mem, positions, w_in_even, w_out_even, diff_lambda, diff_subln_g, conv_w, conv_b, conv_ln_g,
           conv_ln_b, w_in_odd, w_out_odd, xa_wq, xa_wkv, xa_wo, ffn_w_in, ffn_w_out, ln_g, ln_b):
    B, S, D = x.shape
    M = mem.shape[1]
    x = x.reshape(B * S, D)
    mem2 = mem.reshape(B * M, D)
    for layer in range(DEPTH):
        j = layer // 2
        if layer % 2 == 0:
            lam_init = 0.8 - 0.6 * math.exp(-0.3 * layer)
            x = _even_mixer(x, positions, B, w_in_even[j], w_out_even[j], diff_lambda[j], diff_subln_g[j],
                            conv_w[j], conv_b[j], conv_ln_g[j], conv_ln_b[j], lam_init,
                            ln_g[layer, 0], ln_b[layer, 0])
        else:
            x = _odd_mixer(x, positions, B, w_in_odd[j], w_out_odd[j], ln_g[layer, 0], ln_b[layer, 0])
        kv = _inproj(mem2, xa_wkv[layer].astype(BF16), None, (), [-1] * (2 * D // LANES), BF16,
                     tm=512, tn=512, name=f"xa_kvproj_{layer}")
        o = _xattn(x, xa_wq[layer].astype(BF16), kv, B, name=f"xattn_{layer}")
        x = _proj_ln(o, xa_wo[layer].astype(BF16), x, ln_g[layer, 1], ln_b[layer, 1], name=f"xa_outproj_ln_{layer}")
        x = _ffn(x, ffn_w_in[layer].astype(BF16), ffn_w_out[layer].astype(BF16),
                 ln_g[layer, 2], ln_b[layer, 2], name=f"ffn_{layer}")
    return x.reshape(B, S, D)
```

```python
import functools
import math

import jax
import jax.numpy as jnp
from jax import lax
from jax.experimental import pallas as pl
from jax.experimental.pallas import tpu as pltpu

F32 = jnp.float32
BF16 = jnp.bfloat16

D_MODEL = 2048
DEPTH = 2
ALPHA = (2 * DEPTH) ** 0.25
LN_EPS = 1e-5
ROPE_THETA = 500000.0
ROPE_FRAC = 4

DIFF_HEADS = 8
DIFF_HEAD_DIM = 64
DIFF_V_DIM = 2 * DIFF_HEAD_DIM
DIFF_WIDTH = DIFF_HEADS * DIFF_V_DIM
CONV_CH = D_MODEL - DIFF_WIDTH
CONV_WIDTH = 31
EVEN_IN = 3 * DIFF_WIDTH + 2 * CONV_CH

DSA_HEADS = 16
DSA_KV_HEADS = 4
DSA_HEAD_DIM = 128
DSA_REP = DSA_HEADS // DSA_KV_HEADS
IDX_HEADS = 4
IDX_DIM = 64
TOPK_MAX = 256
DSA_Q = DSA_HEADS * DSA_HEAD_DIM
DSA_KV = DSA_KV_HEADS * DSA_HEAD_DIM
IDX_Q = IDX_HEADS * IDX_DIM
ODD_MAIN = DSA_Q + 2 * DSA_KV
IDX_PAD = 512

XA_HEADS = 4
XA_HEAD_DIM = D_MODEL // XA_HEADS
D_FF = -(-(8 * D_MODEL) // (3 * 256)) * 256

LANES = 128
NEG = -1e30
VMEM_LIMIT = 56 * 1024 * 1024


def _cparams(sem):
    return pltpu.CompilerParams(dimension_semantics=sem, vmem_limit_bytes=VMEM_LIMIT)


def _layer_norm_rows(y, g, b):
    mu = jnp.mean(y, axis=-1, keepdims=True)
    d = y - mu
    var = jnp.mean(d * d, axis=-1, keepdims=True)
    return d * lax.rsqrt(var + LN_EPS) * g + b


def _rot_tables(positions, head_dim, active_lanes=LANES):
    rot = head_dim // ROPE_FRAC
    half = rot // 2
    inv_freq = 1.0 / (ROPE_THETA ** (jnp.arange(half, dtype=F32) / half))
    ang = positions.reshape(-1, 1).astype(F32) * inv_freq
    cos, sin = jnp.cos(ang), jnp.sin(ang)
    lane = jnp.arange(LANES)
    d = lane % head_dim
    f = d % half
    live = lane < active_lanes
    cos_l = jnp.take(cos, f, axis=1)
    sin_l = jnp.take(sin, f, axis=1)
    c = jnp.where((d < rot) & live, cos_l, 1.0)
    s1 = jnp.where((d >= half) & (d < rot) & live, sin_l, 0.0)
    s2 = jnp.where((d < half) & live, -sin_l, 0.0)
    return jnp.stack([c, s1, s2]), half


def _inproj_kernel(x_ref, w_ref, tab_ref, o_ref, xb_ref, *, variants, shifts):
    j = pl.program_id(1)

    @pl.when(j == 0)
    def _():
        xb_ref[...] = x_ref[...].astype(BF16)

    acc = jnp.dot(xb_ref[...], w_ref[...], preferred_element_type=F32)

    for (j_lo, j_hi), types in variants:
        @pl.when((j >= j_lo) & (j < j_hi))
        def _(types=types):
            for c, (p, scale) in enumerate(types):
                blk = acc[:, c * LANES:(c + 1) * LANES]
                if p >= 0:
                    sh = shifts[p]
                    blk = (blk * tab_ref[3 * p]
                           + pltpu.roll(blk, sh, 1) * tab_ref[3 * p + 1]
                           + pltpu.roll(blk, LANES - sh, 1) * tab_ref[3 * p + 2])
                if scale != 1.0:
                    blk = blk * scale
                o_ref[:, c * LANES:(c + 1) * LANES] = blk.astype(o_ref.dtype)


def _inproj(x, w, tabs, shifts, chunk_types, out_dtype, *, tm, tn, name, chunk_scales=None):
    T, K = x.shape
    N = w.shape[1]
    tm = min(tm, T)
    assert T % tm == 0 and N % tn == 0 and len(chunk_types) == N // LANES
    if chunk_scales is None:
        chunk_scales = [1.0] * len(chunk_types)
    chunk_types = list(zip(chunk_types, chunk_scales))
    per = tn // LANES
    tile_types = [tuple(chunk_types[t * per:(t + 1) * per]) for t in range(N // tn)]
    variants = []
    for t, ty in enumerate(tile_types):
        if variants and variants[-1][1] == ty and variants[-1][0][1] == t:
            variants[-1] = ((variants[-1][0][0], t + 1), ty)
        else:
            variants.append(((t, t + 1), ty))
    if tabs is None:
        tabs = jnp.zeros((3, T, LANES), F32)
    P3 = tabs.shape[0]
    return pl.pallas_call(
        functools.partial(_inproj_kernel, variants=tuple(variants), shifts=tuple(shifts)),
        grid=(T // tm, N // tn),
        in_specs=[pl.BlockSpec((tm, K), lambda i, j: (i, 0)),
                  pl.BlockSpec((K, tn), lambda i, j: (0, j)),
                  pl.BlockSpec((P3, tm, LANES), lambda i, j: (0, i, 0))],
        out_specs=pl.BlockSpec((tm, tn), lambda i, j: (i, j)),
        out_shape=jax.ShapeDtypeStruct((T, N), out_dtype),
        scratch_shapes=[pltpu.VMEM((tm, K), BF16)],
        compiler_params=_cparams(("parallel", "arbitrary")),
        name=name,
    )(x, w, tabs)


def _proj_ln_kernel(*refs, widths):
    h_refs = refs[:len(widths)]
    w_ref, x_ref, g_ref, b_ref, o_ref = refs[len(widths):]
    y = ALPHA * x_ref[...]
    k0 = 0
    for h_ref, kw in zip(h_refs, widths):
        y = y + jnp.dot(h_ref[...], w_ref[k0:k0 + kw, :], preferred_element_type=F32)
        k0 += kw
    o_ref[...] = _layer_norm_rows(y, g_ref[...], b_ref[...])


def _proj_ln(hs, w, x, g, b, *, tm=512, name):
    T = x.shape[0]
    K, D = w.shape
    widths = tuple(h.shape[1] for h in hs)
    tm = min(tm, T)
    assert T % tm == 0 and sum(widths) == K
    return pl.pallas_call(
        functools.partial(_proj_ln_kernel, widths=widths),
        grid=(T // tm,),
        in_specs=[pl.BlockSpec((tm, kw), lambda i: (i, 0)) for kw in widths] + [
                  pl.BlockSpec((K, D), lambda i: (0, 0), pipeline_mode=pl.Buffered(1)),
                  pl.BlockSpec((tm, D), lambda i: (i, 0)),
                  pl.BlockSpec((1, D), lambda i: (0, 0)),
                  pl.BlockSpec((1, D), lambda i: (0, 0))],
        out_specs=pl.BlockSpec((tm, D), lambda i: (i, 0)),
        out_shape=jax.ShapeDtypeStruct((T, D), F32),
        compiler_params=_cparams(("parallel",)),
        name=name,
    )(*hs, w, x, g.reshape(1, D), b.reshape(1, D))


def _ffn_kernel(x_ref, wg_ref, wu_ref, wo_ref, g_ref, b_ref, o_ref, xb_ref, acc_ref):
    f = pl.program_id(1)

    @pl.when(f == 0)
    def _():
        xb_ref[...] = x_ref[...].astype(BF16)
        acc_ref[...] = jnp.zeros_like(acc_ref)

    xb = xb_ref[...]
    gate = jnp.dot(xb, wg_ref[...], preferred_element_type=F32)
    up = jnp.dot(xb, wu_ref[...], preferred_element_type=F32)
    h = (gate * jax.nn.sigmoid(gate) * up).astype(BF16)
    acc_ref[...] += jnp.dot(h, wo_ref[...], preferred_element_type=F32)

    @pl.when(f == pl.num_programs(1) - 1)
    def _():
        y = ALPHA * x_ref[...] + acc_ref[...]
        o_ref[...] = _layer_norm_rows(y, g_ref[...], b_ref[...])


def _ffn(x, w_in, w_out, g, b, *, tm=512, tf=512, name):
    T, D = x.shape
    dff = w_out.shape[0]
    tm = min(tm, T)
    assert T % tm == 0 and dff % tf == 0
    nf = dff // tf
    return pl.pallas_call(
        _ffn_kernel,
        grid=(T // tm, nf),
        in_specs=[pl.BlockSpec((tm, D), lambda i, f: (i, 0)),
                  pl.BlockSpec((D, tf), lambda i, f: (0, f)),
                  pl.BlockSpec((D, tf), lambda i, f: (0, f + nf)),
                  pl.BlockSpec((tf, D), lambda i, f: (f, 0)),
                  pl.BlockSpec((1, D), lambda i, f: (0, 0)),
                  pl.BlockSpec((1, D), lambda i, f: (0, 0))],
        out_specs=pl.BlockSpec((tm, D), lambda i, f: (i, 0)),
        out_shape=jax.ShapeDtypeStruct((T, D), F32),
        scratch_shapes=[pltpu.VMEM((tm, D), BF16), pltpu.VMEM((tm, D), F32)],
        compiler_params=_cparams(("parallel", "arbitrary")),
        name=name,
    )(x, w_in, w_in, w_out, g.reshape(1, D), b.reshape(1, D))


def _xattn_kernel(x_ref, wq_ref, k_ref, v_ref, o_ref):
    xb = x_ref[...].astype(BF16)
    q = jnp.dot(xb, wq_ref[...], preferred_element_type=F32).astype(BF16)
    scale = XA_HEAD_DIM ** -0.5
    for h in range(XA_HEADS):
        sl = slice(h * XA_HEAD_DIM, (h + 1) * XA_HEAD_DIM)
        s = lax.dot_general(q[:, sl], k_ref[:, sl], (((1,), (1,)), ((), ())),
                            preferred_element_type=F32) * scale
        m = jnp.max(s, axis=-1, keepdims=True)
        p = jnp.exp(s - m)
        l = jnp.sum(p, axis=-1, keepdims=True)
        p = (p / l).astype(BF16)
        o = jnp.dot(p, v_ref[:, sl], preferred_element_type=F32)
        o_ref[:, sl] = o.astype(o_ref.dtype)


def _xattn(x, wq, kv, batch, *, tm=512, name):
    T, D = x.shape
    S = T // batch
    M = kv.shape[0] // batch
    tm = min(tm, S)
    ns = S // tm
    return pl.pallas_call(
        _xattn_kernel,
        grid=(batch, ns),
        in_specs=[pl.BlockSpec((tm, D), lambda b, i: (b * ns + i, 0)),
                  pl.BlockSpec((D, D), lambda b, i: (0, 0)),
                  pl.BlockSpec((M, D), lambda b, i: (b, 0)),
                  pl.BlockSpec((M, D), lambda b, i: (b, 1))],
        out_specs=pl.BlockSpec((tm, D), lambda b, i: (b * ns + i, 0)),
        out_shape=jax.ShapeDtypeStruct((T, D), BF16),
        compiler_params=_cparams(("parallel", "arbitrary")),
        name=name,
    )(x, wq, kv, kv)


NT_DIMS = (((1,), (1,)), ((), ()))
LOG2E = math.log2(math.e)


def _causal_pairs(nq, ratio):
    qi = [i for i in range(nq) for _ in range((i + 1) * ratio)]
    kj = [j for i in range(nq) for j in range((i + 1) * ratio)]
    return jnp.asarray(qi, jnp.int32), jnp.asarray(kj, jnp.int32)


def _flash_update(q, k, v_ext, m_ref, acc_ref, idx, rows, bias=None, mask=None):
    s = lax.dot_general(q, k, NT_DIMS, preferred_element_type=F32)
    if bias is not None:
        s = s + bias
    if mask is not None:
        s = jnp.where(mask, s, NEG)
    chunks = [s[:, j * LANES:(j + 1) * LANES] for j in range(s.shape[1] // LANES)]
    mc = chunks[0]
    for ch in chunks[1:]:
        mc = jnp.maximum(mc, ch)
    m_prev = m_ref[idx, rows, :]
    m_new = jnp.maximum(m_prev, jnp.max(mc, axis=1, keepdims=True))
    p = jnp.concatenate([jnp.exp2(ch - m_new).astype(BF16) for ch in chunks], axis=1)
    alpha = jnp.exp2(m_prev - m_new)
    pv = jnp.dot(p, v_ext, preferred_element_type=F32)
    acc_ref[idx, rows, :] = jnp.concatenate([alpha, alpha], axis=1) * acc_ref[idx, rows, :] + pv
    m_ref[idx, rows, :] = m_new


def _flash_init(m_ref, acc_ref):
    m_ref[...] = jnp.full_like(m_ref, -jnp.inf)
    acc_ref[...] = jnp.zeros_like(acc_ref)


def _with_ones(v):
    return jnp.concatenate([v, jnp.ones_like(v)], axis=1)


def _diff_attn_kernel(lam_ref, q_ref, k_ref, v_ref, g_ref, o_ref, m_ref, acc_ref, *, tq, tk, rq, lam_init):
    qi = pl.program_id(2)
    ratio = tq // tk
    _flash_init(m_ref, acc_ref)

    def step(kj, d):
        k0 = pl.multiple_of(kj * tk, tk)
        k = k_ref[pl.ds(k0, tk), :]
        v_ext = _with_ones(v_ref[pl.ds(k0, tk), :])
        lane = lax.broadcasted_iota(jnp.int32, (rq, LANES), 1)
        for rb in range(tq // rq):
            if d is not None and (rb + 1) * rq <= d * tk:
                continue
            rows = slice(rb * rq, (rb + 1) * rq)
            q = q_ref[rows, :]
            mask = None
            if d is not None and rb * rq < (d + 1) * tk - 1:
                row = rb * rq + lax.broadcasted_iota(jnp.int32, (rq, tk), 0)
                col = d * tk + lax.broadcasted_iota(jnp.int32, (rq, tk), 1)
                mask = col <= row
            for c, sel in enumerate((lane < DIFF_HEAD_DIM, lane >= DIFF_HEAD_DIM)):
                _flash_update(jnp.where(sel, q, jnp.zeros_like(q)), k, v_ext, m_ref, acc_ref, c, rows, mask=mask)

    def below_diagonal(kj, carry):
        step(kj, None)
        return carry

    lax.fori_loop(0, qi * ratio, below_diagonal, 0)
    for d in range(ratio):
        step(qi * ratio + d, d)

    lam = lam_ref[0]
    a = (acc_ref[0, :, :LANES] / acc_ref[0, :, LANES:]
         - lam * (acc_ref[1, :, :LANES] / acc_ref[1, :, LANES:]))
    ms = jnp.mean(a * a, axis=-1, keepdims=True)
    a = a * lax.rsqrt(ms + LN_EPS) * g_ref[...]
    o_ref[...] = (a * (1.0 - lam_init)).astype(o_ref.dtype)


def _diff_attn(proj, lam, subln_g, batch, lam_init, *, tq=1024, tk=1024, rq=128, name):
    T = proj.shape[0]
    S = T // batch
    tq, tk = min(tq, S), min(tk, S)
    rq = min(rq, tq)
    assert S % tq == 0 and tq % tk == 0 and tq % rq == 0
    nq = S // tq
    H = DIFF_HEADS
    return pl.pallas_call(
        functools.partial(_diff_attn_kernel, tq=tq, tk=tk, rq=rq, lam_init=lam_init),
        grid=(batch, H, nq),
        in_specs=[pl.BlockSpec(memory_space=pltpu.SMEM),
                  pl.BlockSpec((tq, LANES), lambda b, h, i: (b * nq + i, h)),
                  pl.BlockSpec((S, LANES), lambda b, h, i: (b, H + h)),
                  pl.BlockSpec((S, LANES), lambda b, h, i: (b, 2 * H + h)),
                  pl.BlockSpec((1, LANES), lambda b, h, i: (0, 0))],
        out_specs=pl.BlockSpec((tq, LANES), lambda b, h, i: (b * nq + i, h)),
        out_shape=jax.ShapeDtypeStruct((T, DIFF_WIDTH), BF16),
        scratch_shapes=[pltpu.VMEM((2, tq, LANES), F32), pltpu.VMEM((2, tq, 2 * LANES), F32)],
        compiler_params=_cparams(("parallel", "parallel", "arbitrary")),
        name=name,
    )(lam.reshape(1).astype(F32), proj, proj, proj, subln_g.reshape(1, LANES).astype(F32))


CONV_HALO = 32
CONV_ROWS = 64


def _conv_kernel(val_ref, gate_ref, hval_ref, hgate_ref, cw_ref, cb_ref, g_ref, b_ref, o_ref,
                 u_ref, c_ref, *, tr):
    i = pl.program_id(1)
    u_ref[CONV_HALO:, :] = val_ref[...].astype(F32) * jax.nn.sigmoid(gate_ref[...].astype(F32))
    halo = hval_ref[...].astype(F32) * jax.nn.sigmoid(hgate_ref[...].astype(F32))
    u_ref[:CONV_HALO, :] = jnp.where(i > 0, halo, 0.0)
    base = CONV_HALO - (CONV_WIDTH - 1)
    for r in range(tr // CONV_ROWS):
        for c in range(CONV_CH // LANES):
            cs = slice(c * LANES, (c + 1) * LANES)
            acc = jnp.broadcast_to(cb_ref[:, cs], (CONV_ROWS, LANES))
            for w in range(CONV_WIDTH):
                r0 = r * CONV_ROWS + base + w
                acc = acc + u_ref[r0:r0 + CONV_ROWS, cs] * cw_ref[w:w + 1, cs]
            c_ref[r * CONV_ROWS:(r + 1) * CONV_ROWS, cs] = acc
    y = _layer_norm_rows(c_ref[...], g_ref[...], b_ref[...])
    o_ref[...] = (y * jax.nn.sigmoid(y)).astype(o_ref.dtype)


def _conv_module(proj, conv_w, conv_b, ln_g, ln_b, batch, *, tr=256, name):
    T = proj.shape[0]
    S = T // batch
    tr = min(tr, S)
    ns = S // tr
    vb = (3 * DIFF_WIDTH) // CONV_CH
    hpb = tr // CONV_HALO

    def halo_idx(col):
        return lambda b, i: (jnp.maximum((b * ns + i) * hpb - 1, 0), col)

    vec = lambda a: a.reshape(1, CONV_CH).astype(F32)
    return pl.pallas_call(
        functools.partial(_conv_kernel, tr=tr),
        grid=(batch, ns),
        in_specs=[pl.BlockSpec((tr, CONV_CH), lambda b, i: (b * ns + i, vb)),
                  pl.BlockSpec((tr, CONV_CH), lambda b, i: (b * ns + i, vb + 1)),
                  pl.BlockSpec((CONV_HALO, CONV_CH), halo_idx(vb)),
                  pl.BlockSpec((CONV_HALO, CONV_CH), halo_idx(vb + 1)),
                  pl.BlockSpec((CONV_WIDTH, CONV_CH), lambda b, i: (0, 0)),
                  pl.BlockSpec((1, CONV_CH), lambda b, i: (0, 0)),
                  pl.BlockSpec((1, CONV_CH), lambda b, i: (0, 0)),
                  pl.BlockSpec((1, CONV_CH), lambda b, i: (0, 0))],
        out_specs=pl.BlockSpec((tr, CONV_CH), lambda b, i: (b * ns + i, 0)),
        out_shape=jax.ShapeDtypeStruct((T, CONV_CH), BF16),
        scratch_shapes=[pltpu.VMEM((CONV_HALO + tr, CONV_CH), F32), pltpu.VMEM((tr, CONV_CH), F32)],
        compiler_params=_cparams(("parallel", "arbitrary")),
        name=name,
    )(proj, proj, proj, proj, conv_w.astype(F32), vec(conv_b), vec(ln_g), vec(ln_b))


IDX_CHUNK = 1024
COUNT_CHUNK = 1024
KEY16_LO = -32640
KEY16_HI = 0x7F80
KEY16_BITS = 16


def _key_to_f32(key):
    bits = jnp.where(key < 0, key ^ jnp.int32(0x7FFFFFFF), key)
    return pltpu.bitcast(bits, F32)


def _indexer_kernel(q_ref, k_ref, o_ref, sc_ref, g_ref, *, tq, topk, seq):
    i = pl.program_id(1)
    q0 = i * tq
    nch = (q0 + tq + IDX_CHUNK - 1) // IDX_CHUNK
    n_all = seq // IDX_CHUNK
    lane = lax.broadcasted_iota(jnp.int32, (tq, LANES), 1)

    qh = []
    for blk in range(IDX_Q // LANES):
        qb = q_ref[:, blk * LANES:(blk + 1) * LANES]
        qh.append(jnp.where(lane < IDX_DIM, qb, 0.0).astype(BF16))
        qh.append(jnp.where(lane < IDX_DIM, pltpu.roll(qb, IDX_DIM, 1), 0.0).astype(BF16))
    wcol = IDX_Q + IDX_DIM
    wscale = IDX_HEADS ** -0.5 * IDX_DIM ** -0.5
    wts = [q_ref[:, wcol + h:wcol + h + 1] * wscale for h in range(IDX_HEADS)]

    def score_chunk(c, masked):
        k0 = pl.multiple_of(c * IDX_CHUNK, IDX_CHUNK)
        kc = k_ref[pl.ds(k0, IDX_CHUNK), :]
        sc = None
        for h in range(IDX_HEADS):
            r = lax.dot_general(qh[h], kc, (((1,), (1,)), ((), ())), preferred_element_type=F32)
            t = wts[h] * jnp.maximum(r, 0.0)
            sc = t if sc is None else sc + t
        if masked:
            row = q0 + lax.broadcasted_iota(jnp.int32, (tq, IDX_CHUNK), 0)
            col = k0 + lax.broadcasted_iota(jnp.int32, (tq, IDX_CHUNK), 1)
            sc = jnp.where(col <= row, sc, -jnp.inf)
        sc_ref[:, pl.ds(k0, IDX_CHUNK)] = sc
        hi_bits = pltpu.bitcast(sc, jnp.int32) & jnp.int32(-(1 << KEY16_BITS))
        g_ref[:, pl.ds(k0, IDX_CHUNK)] = pltpu.bitcast(hi_bits, F32).astype(BF16)

    def unmasked_body(c, carry):
        score_chunk(c, False)
        return carry

    lax.fori_loop(0, nch - 1, unmasked_body, 0)
    score_chunk(nch - 1, True)

    per = COUNT_CHUNK // IDX_CHUNK
    n_cnt = (nch + per - 1) // per

    def pad(c, carry):
        k0 = pl.multiple_of(c * IDX_CHUNK, IDX_CHUNK)
        sc_ref[:, pl.ds(k0, IDX_CHUNK)] = jnp.full((tq, IDX_CHUNK), -jnp.inf, F32)
        g_ref[:, pl.ds(k0, IDX_CHUNK)] = jnp.full((tq, IDX_CHUNK), -jnp.inf, BF16)
        return carry
    lax.fori_loop(nch, n_cnt * per, pad, 0)

    def count_in(ref, dtype, pred):
        one = jnp.ones((tq, LANES), dtype)
        zero = jnp.zeros((tq, LANES), dtype)

        def body(c, acc):
            k0 = pl.multiple_of(c * COUNT_CHUNK, COUNT_CHUNK)
            for jj in range(COUNT_CHUNK // LANES):
                blk = ref[:, pl.ds(k0 + jj * LANES, LANES)]
                acc = acc + jnp.where(pred(blk), one, zero)
            return acc
        acc = lax.fori_loop(0, n_cnt, body, zero)
        return jnp.broadcast_to(jnp.sum(acc.astype(F32), axis=1, keepdims=True), (tq, LANES))

    def count(pred):
        return count_in(sc_ref, F32, pred)

    kf = float(topk)
    def bisect16(_, carry):
        lo, hi, cnt_lo = carry
        mid = (lo + hi) >> 1
        key = jnp.where(mid < 0, (mid << KEY16_BITS) | jnp.int32(0xFFFF), mid << KEY16_BITS)
        cand = _key_to_f32(key).astype(BF16)
        cnt = count_in(g_ref, BF16, lambda blk: blk >= cand)
        ok = cnt >= kf
        return jnp.where(ok, mid, lo), jnp.where(ok, hi, mid), jnp.where(ok, cnt, cnt_lo)

    lo16 = jnp.full((tq, LANES), KEY16_LO, jnp.int32)
    hi16 = jnp.full((tq, LANES), KEY16_HI, jnp.int32)
    lo16, _, cnt_lo = lax.fori_loop(0, KEY16_BITS, bisect16, (lo16, hi16, jnp.zeros((tq, LANES), F32)))

    def unresolved(carry):
        lo, hi, _ = carry
        return jnp.max(hi - lo) > 1

    def bisect32(carry):
        lo, hi, cnt_lo = carry
        mid = lo + ((hi - lo) >> 1)
        cand = _key_to_f32(mid)
        cnt = count(lambda blk: blk >= cand)
        ok = cnt >= kf
        hi = jnp.where(cnt == kf, mid + 1, jnp.where(ok, hi, mid))
        return jnp.where(ok, mid, lo), hi, jnp.where(ok, cnt, cnt_lo)

    lo32 = lo16 << KEY16_BITS
    at_start = count(lambda blk: blk > _key_to_f32(lo32)) < kf
    hi32 = jnp.where(at_start, lo32 + 1, (lo16 + 1) << KEY16_BITS)
    lo, _, n_ge = lax.while_loop(unresolved, bisect32, (lo32, hi32, cnt_lo))
    thr = _key_to_f32(lo)
    has_tie = jnp.max(n_ge) > kf
    thr_c = thr[:, :1]

    @pl.when(jnp.logical_not(has_tie))
    def _():
        def body(c, carry):
            k0 = pl.multiple_of(c * IDX_CHUNK, IDX_CHUNK)
            sc = sc_ref[:, pl.ds(k0, IDX_CHUNK)]
            o_ref[0, :, pl.ds(k0, IDX_CHUNK)] = jnp.where(sc >= thr_c, 0.0, NEG).astype(o_ref.dtype)
            return carry
        lax.fori_loop(0, nch, body, 0)

    @pl.when(has_tie)
    def _():
        r_i = lax.broadcasted_iota(jnp.int32, (IDX_CHUNK, IDX_CHUNK), 0)
        c_i = lax.broadcasted_iota(jnp.int32, (IDX_CHUNK, IDX_CHUNK), 1)
        upper = jnp.where(r_i < c_i, 1.0, 0.0).astype(BF16)
        need_c = (kf - count(lambda blk: blk > thr))[:, :1]
        tie_c = jnp.where(n_ge > kf, 1.0, 0.0)[:, :1]

        def pending(carry):
            c, seen = carry
            return (c < nch) & (jnp.max(tie_c * (need_c - seen)) > 0.0)

        def body(carry):
            c, seen = carry
            k0 = pl.multiple_of(c * IDX_CHUNK, IDX_CHUNK)
            sc = sc_ref[:, pl.ds(k0, IDX_CHUNK)]
            eq = jnp.where(sc == thr_c, 1.0, 0.0)
            before = seen + jnp.dot(eq.astype(BF16), upper, preferred_element_type=F32)
            keep = (sc > thr_c) | ((sc == thr_c) & (before < need_c))
            o_ref[0, :, pl.ds(k0, IDX_CHUNK)] = jnp.where(keep, 0.0, NEG).astype(o_ref.dtype)
            return c + 1, seen + jnp.sum(eq, axis=1, keepdims=True)
        c_done, _ = lax.while_loop(pending, body, (jnp.int32(0), jnp.zeros((tq, 1), F32)))

        def rest(c, carry):
            k0 = pl.multiple_of(c * IDX_CHUNK, IDX_CHUNK)
            sc = sc_ref[:, pl.ds(k0, IDX_CHUNK)]
            keep = (sc > thr_c) | ((sc == thr_c) & (tie_c < 0.5))
            o_ref[0, :, pl.ds(k0, IDX_CHUNK)] = jnp.where(keep, 0.0, NEG).astype(o_ref.dtype)
            return carry
        lax.fori_loop(c_done, nch, rest, 0)

    def fill(c, carry):
        k0 = pl.multiple_of(c * IDX_CHUNK, IDX_CHUNK)
        o_ref[0, :, pl.ds(k0, IDX_CHUNK)] = jnp.full((tq, IDX_CHUNK), NEG, o_ref.dtype)
        return carry
    lax.fori_loop(nch, n_all, fill, 0)


def _indexer(idx, kidx, batch, *, tq=128, name):
    T = idx.shape[0]
    S = T // batch
    tq = min(tq, S)
    ns = S // tq
    topk = min(TOPK_MAX, S // 4)
    assert S % COUNT_CHUNK == 0 and IDX_CHUNK % tq == 0
    return pl.pallas_call(
        functools.partial(_indexer_kernel, tq=tq, topk=topk, seq=S),
        grid=(batch, ns),
        in_specs=[pl.BlockSpec((tq, IDX_PAD), lambda b, i: (b * ns + i, 0)),
                  pl.BlockSpec((S, LANES), lambda b, i: (b, 0))],
        out_specs=pl.BlockSpec((1, tq, S), lambda b, i: (b, i, 0)),
        out_shape=jax.ShapeDtypeStruct((batch, S, S), BF16),
        scratch_shapes=[pltpu.VMEM((tq, S), F32), pltpu.VMEM((tq, S), BF16)],
        compiler_params=_cparams(("parallel", "arbitrary")),
        name=name,
    )(idx, kidx)


def _dsa_attn_kernel(qi_ref, kj_ref, q_ref, k_ref, v_ref, bias_ref, o_ref, m_ref, acc_ref, *, tq, tk, rq):
    p_id = pl.program_id(2)
    qi = qi_ref[p_id]
    kj = kj_ref[p_id]
    ratio = tq // tk

    @pl.when(kj == 0)
    def _():
        _flash_init(m_ref, acc_ref)

    def step(d):
        k = k_ref[...]
        v_ext = _with_ones(v_ref[...])
        for rb in range(tq // rq):
            if d is not None and (rb + 1) * rq <= d * tk:
                continue
            rows = slice(rb * rq, (rb + 1) * rq)
            bias = bias_ref[0, rows, :].astype(F32)
            for r in range(DSA_REP):
                q = q_ref[rows, r * DSA_HEAD_DIM:(r + 1) * DSA_HEAD_DIM]
                _flash_update(q, k, v_ext, m_ref, acc_ref, r, rows, bias=bias)

    @pl.when(kj < qi * ratio)
    def _():
        step(None)

    for d in range(ratio):
        @pl.when(kj == qi * ratio + d)
        def _(d=d):
            step(d)

    @pl.when(kj == (qi + 1) * ratio - 1)
    def _():
        for r in range(DSA_REP):
            o = acc_ref[r, :, :DSA_HEAD_DIM] / acc_ref[r, :, DSA_HEAD_DIM:]
            o_ref[:, r * DSA_HEAD_DIM:(r + 1) * DSA_HEAD_DIM] = o.astype(o_ref.dtype)


def _dsa_attn(qkv, bias, batch, *, tq=1024, tk=1024, rq=128, name):
    T = qkv.shape[0]
    S = T // batch
    tq, tk = min(tq, S), min(tk, S)
    rq = min(rq, tq)
    assert S % tq == 0 and tq % tk == 0 and tq % rq == 0
    nq, nk = S // tq, S // tk
    qi, kj = _causal_pairs(nq, tq // tk)
    G = DSA_KV_HEADS
    gw = DSA_REP * DSA_HEAD_DIM
    kb = DSA_Q // DSA_HEAD_DIM
    grid_spec = pltpu.PrefetchScalarGridSpec(
        num_scalar_prefetch=2,
        grid=(batch, G, int(qi.shape[0])),
        in_specs=[pl.BlockSpec((tq, gw), lambda b, g, p, qi, kj: (b * nq + qi[p], g)),
                  pl.BlockSpec((tk, DSA_HEAD_DIM), lambda b, g, p, qi, kj: (b * nk + kj[p], kb + g)),
                  pl.BlockSpec((tk, DSA_HEAD_DIM), lambda b, g, p, qi, kj: (b * nk + kj[p], kb + G + g)),
                  pl.BlockSpec((1, tq, tk), lambda b, g, p, qi, kj: (b, qi[p], kj[p]))],
        out_specs=pl.BlockSpec((tq, gw), lambda b, g, p, qi, kj: (b * nq + qi[p], g)),
        scratch_shapes=[pltpu.VMEM((DSA_REP, tq, DSA_HEAD_DIM), F32),
                        pltpu.VMEM((DSA_REP, tq, 2 * DSA_HEAD_DIM), F32)])
    return pl.pallas_call(
        functools.partial(_dsa_attn_kernel, tq=tq, tk=tk, rq=rq),
        grid_spec=grid_spec,
        out_shape=jax.ShapeDtypeStruct((T, DSA_Q), BF16),
        compiler_params=_cparams(("parallel", "parallel", "arbitrary")),
        name=name,
    )(qi, kj, qkv, qkv, qkv, bias)


def _even_mixer(x, positions, batch, w_in, w_out, lam_p, subln_g, conv_w, conv_b, conv_ln_g, conv_ln_b,
                lam_init, ln_g, ln_b):
    tabs, half = _rot_tables(positions, DIFF_HEAD_DIM)
    n_rot = (2 * DIFF_WIDTH) // LANES
    chunk_types = [0] * n_rot + [-1] * ((EVEN_IN - 2 * DIFF_WIDTH) // LANES)
    q_scale = DIFF_HEAD_DIM ** -0.5 * LOG2E
    chunk_scales = [q_scale] * (DIFF_WIDTH // LANES) + [1.0] * ((EVEN_IN - DIFF_WIDTH) // LANES)
    proj = _inproj(x, w_in.astype(BF16), tabs, (half,), chunk_types, BF16, tm=1024, tn=1024, name="even_inproj",
                   chunk_scales=chunk_scales)
    lp = lam_p.astype(F32)
    lam = jnp.exp(jnp.sum(lp[0] * lp[1])) - jnp.exp(jnp.sum(lp[2] * lp[3])) + lam_init
    a = _diff_attn(proj, lam, subln_g, batch, lam_init, name="diff_attn")
    c = _conv_module(proj, conv_w, conv_b, conv_ln_g, conv_ln_b, batch, name="conv_module")
    return _proj_ln((a, c), w_out.astype(BF16), x, ln_g, ln_b, name="even_outproj_ln")


def _odd_mixer(x, positions, batch, w_in, w_out, ln_g, ln_b):
    T = x.shape[0]
    tabs_b, half_b = _rot_tables(positions, DSA_HEAD_DIM)
    n_rot = (DSA_Q + DSA_KV) // LANES
    chunk_types = [0] * n_rot + [-1] * (DSA_KV // LANES)
    w_main = w_in[:, :ODD_MAIN].astype(BF16)
    q_scale = DSA_HEAD_DIM ** -0.5 * LOG2E
    chunk_scales = [q_scale] * (DSA_Q // LANES) + [1.0] * ((ODD_MAIN - DSA_Q) // LANES)
    qkv = _inproj(x, w_main, tabs_b, (half_b,), chunk_types, BF16, tm=1024, tn=1024, name="odd_inproj",
                  chunk_scales=chunk_scales)
    w_idx = jnp.pad(w_in[:, ODD_MAIN:], ((0, 0), (0, IDX_PAD - (w_in.shape[1] - ODD_MAIN)))).astype(BF16)
    tabs_a, half_a = _rot_tables(positions, IDX_DIM)
    tabs_ah, _ = _rot_tables(positions, IDX_DIM, active_lanes=IDX_DIM)
    idx_types = [0] * (IDX_Q // LANES) + [1] + [-1] * ((IDX_PAD - IDX_Q) // LANES - 1)
    idx = _inproj(x, w_idx, jnp.concatenate([tabs_a, tabs_ah]), (half_a, half_a), idx_types, F32,
                  tm=1024, tn=IDX_PAD, name="idx_inproj")
    kidx = idx[:, IDX_Q:IDX_Q + LANES].astype(BF16)
    bias = _indexer(idx, kidx, batch, name="indexer")
    o = _dsa_attn(qkv, bias, batch, name="dsa_attn")
    return _proj_ln((o,), w_out.astype(BF16), x, ln_g, ln_b, name="odd_outproj_ln")


def kernel(x, mem, positions, w_in_even, w_out_even, diff_lambda, diff_subln_g, conv_w, conv_b, conv_ln_g,
           conv_ln_b, w_in_odd, w_out_odd, xa_wq, xa_wkv, xa_wo, ffn_w_in, ffn_w_out, ln_g, ln_b):
    B, S, D = x.shape
    M = mem.shape[1]
    x = x.reshape(B * S, D)
    mem2 = mem.reshape(B * M, D)
    for layer in range(DEPTH):
        j = layer // 2
        if layer % 2 == 0:
            lam_init = 0.8 - 0.6 * math.exp(-0.3 * layer)
            x = _even_mixer(x, positions, B, w_in_even[j], w_out_even[j], diff_lambda[j], diff_subln_g[j],
                            conv_w[j], conv_b[j], conv_ln_g[j], conv_ln_b[j], lam_init,
                            ln_g[layer, 0], ln_b[layer, 0])
        else:
            x = _odd_mixer(x, positions, B, w_in_odd[j], w_out_odd[j], ln_g[layer, 0], ln_b[layer, 0])
        kv = _inproj(mem2, xa_wkv[layer].astype(BF16), None, (), [-1] * (2 * D // LANES), BF16,
                     tm=512, tn=512, name=f"xa_kvproj_{layer}")
        o = _xattn(x, xa_wq[layer].astype(BF16), kv, B, name=f"xattn_{layer}")
        x = _proj_ln((o,), xa_wo[layer].astype(BF16), x, ln_g[layer, 1], ln_b[layer, 1],
                     name=f"xa_outproj_ln_{layer}")
        x = _ffn(x, ffn_w_in[layer].astype(BF16), ffn_w_out[layer].astype(BF16),
                 ln_g[layer, 2], ln_b[layer, 2], name=f"ffn_{layer}")
    return x.reshape(B, S, D)
```

```python
import functools
import math

import jax
import jax.numpy as jnp
from jax import lax
from jax.experimental import pallas as pl
from jax.experimental.pallas import tpu as pltpu

F32 = jnp.float32
BF16 = jnp.bfloat16

D_MODEL = 2048
DEPTH = 2
ALPHA = (2 * DEPTH) ** 0.25
LN_EPS = 1e-5
ROPE_THETA = 500000.0
ROPE_FRAC = 4

DIFF_HEADS = 8
DIFF_HEAD_DIM = 64
DIFF_V_DIM = 2 * DIFF_HEAD_DIM
DIFF_WIDTH = DIFF_HEADS * DIFF_V_DIM
CONV_CH = D_MODEL - DIFF_WIDTH
CONV_WIDTH = 31
EVEN_IN = 3 * DIFF_WIDTH + 2 * CONV_CH

DSA_HEADS = 16
DSA_KV_HEADS = 4
DSA_HEAD_DIM = 128
DSA_REP = DSA_HEADS // DSA_KV_HEADS
IDX_HEADS = 4
IDX_DIM = 64
TOPK_MAX = 256
DSA_Q = DSA_HEADS * DSA_HEAD_DIM
DSA_KV = DSA_KV_HEADS * DSA_HEAD_DIM
IDX_Q = IDX_HEADS * IDX_DIM
ODD_MAIN = DSA_Q + 2 * DSA_KV
IDX_PAD = 512

XA_HEADS = 4
XA_HEAD_DIM = D_MODEL // XA_HEADS
D_FF = -(-(8 * D_MODEL) // (3 * 256)) * 256

LANES = 128
SUBLANES = 8
NEG = -1e30
VMEM_LIMIT = 56 * 1024 * 1024


def _cparams(sem):
    return pltpu.CompilerParams(dimension_semantics=sem, vmem_limit_bytes=VMEM_LIMIT)


def _layer_norm_rows(y, g, b):
    mu = jnp.mean(y, axis=-1, keepdims=True)
    d = y - mu
    var = jnp.mean(d * d, axis=-1, keepdims=True)
    return d * lax.rsqrt(var + LN_EPS) * g + b


def _rot_tables(positions, head_dim, active_lanes=LANES):
    rot = head_dim // ROPE_FRAC
    half = rot // 2
    inv_freq = 1.0 / (ROPE_THETA ** (jnp.arange(half, dtype=F32) / half))
    ang = positions.reshape(-1, 1).astype(F32) * inv_freq
    cos, sin = jnp.cos(ang), jnp.sin(ang)
    lane = jnp.arange(LANES)
    d = lane % head_dim
    f = d % half
    live = lane < active_lanes
    cos_l = jnp.take(cos, f, axis=1)
    sin_l = jnp.take(sin, f, axis=1)
    c = jnp.where((d < rot) & live, cos_l, 1.0)
    s1 = jnp.where((d >= half) & (d < rot) & live, sin_l, 0.0)
    s2 = jnp.where((d < half) & live, -sin_l, 0.0)
    return jnp.stack([c, s1, s2]), half


def _inproj_kernel(x_ref, w_ref, tab_ref, o_ref, xb_ref, *, variants, shifts):
    j = pl.program_id(1)

    @pl.when(j == 0)
    def _():
        xb_ref[...] = x_ref[...].astype(BF16)

    acc = jnp.dot(xb_ref[...], w_ref[...], preferred_element_type=F32)

    for (j_lo, j_hi), types in variants:
        @pl.when((j >= j_lo) & (j < j_hi))
        def _(types=types):
            for c, (p, scale) in enumerate(types):
                blk = acc[:, c * LANES:(c + 1) * LANES]
                if p >= 0:
                    sh = shifts[p]
                    blk = (blk * tab_ref[3 * p]
                           + pltpu.roll(blk, sh, 1) * tab_ref[3 * p + 1]
                           + pltpu.roll(blk, LANES - sh, 1) * tab_ref[3 * p + 2])
                if scale != 1.0:
                    blk = blk * scale
                o_ref[:, c * LANES:(c + 1) * LANES] = blk.astype(o_ref.dtype)


def _inproj(x, w, tabs, shifts, chunk_types, out_dtype, *, tm, tn, name, chunk_scales=None):
    T, K = x.shape
    N = w.shape[1]
    tm = min(tm, T)
    assert T % tm == 0 and N % tn == 0 and len(chunk_types) == N // LANES
    if chunk_scales is None:
        chunk_scales = [1.0] * len(chunk_types)
    chunk_types = list(zip(chunk_types, chunk_scales))
    per = tn // LANES
    tile_types = [tuple(chunk_types[t * per:(t + 1) * per]) for t in range(N // tn)]
    variants = []
    for t, ty in enumerate(tile_types):
        if variants and variants[-1][1] == ty and variants[-1][0][1] == t:
            variants[-1] = ((variants[-1][0][0], t + 1), ty)
        else:
            variants.append(((t, t + 1), ty))
    if tabs is None:
        tabs = jnp.zeros((3, T, LANES), F32)
    P3 = tabs.shape[0]
    return pl.pallas_call(
        functools.partial(_inproj_kernel, variants=tuple(variants), shifts=tuple(shifts)),
        grid=(T // tm, N // tn),
        in_specs=[pl.BlockSpec((tm, K), lambda i, j: (i, 0)),
                  pl.BlockSpec((K, tn), lambda i, j: (0, j)),
                  pl.BlockSpec((P3, tm, LANES), lambda i, j: (0, i, 0))],
        out_specs=pl.BlockSpec((tm, tn), lambda i, j: (i, j)),
        out_shape=jax.ShapeDtypeStruct((T, N), out_dtype),
        scratch_shapes=[pltpu.VMEM((tm, K), BF16)],
        compiler_params=_cparams(("parallel", "arbitrary")),
        name=name,
    )(x, w, tabs)


def _proj_ln_kernel(*refs, widths):
    h_refs = refs[:len(widths)]
    w_ref, x_ref, g_ref, b_ref, o_ref = refs[len(widths):]
    y = ALPHA * x_ref[...]
    k0 = 0
    for h_ref, kw in zip(h_refs, widths):
        y = y + jnp.dot(h_ref[...], w_ref[k0:k0 + kw, :], preferred_element_type=F32)
        k0 += kw
    o_ref[...] = _layer_norm_rows(y, g_ref[...], b_ref[...])


def _proj_ln(hs, w, x, g, b, *, tm=512, name):
    T = x.shape[0]
    K, D = w.shape
    widths = tuple(h.shape[1] for h in hs)
    tm = min(tm, T)
    assert T % tm == 0 and sum(widths) == K
    return pl.pallas_call(
        functools.partial(_proj_ln_kernel, widths=widths),
        grid=(T // tm,),
        in_specs=[pl.BlockSpec((tm, kw), lambda i: (i, 0)) for kw in widths] + [
                  pl.BlockSpec((K, D), lambda i: (0, 0), pipeline_mode=pl.Buffered(1)),
                  pl.BlockSpec((tm, D), lambda i: (i, 0)),
                  pl.BlockSpec((1, D), lambda i: (0, 0)),
                  pl.BlockSpec((1, D), lambda i: (0, 0))],
        out_specs=pl.BlockSpec((tm, D), lambda i: (i, 0)),
        out_shape=jax.ShapeDtypeStruct((T, D), F32),
        compiler_params=_cparams(("parallel",)),
        name=name,
    )(*hs, w, x, g.reshape(1, D), b.reshape(1, D))


def _ffn_kernel(x_ref, wg_ref, wu_ref, wo_ref, g_ref, b_ref, o_ref, xb_ref, acc_ref):
    f = pl.program_id(1)

    @pl.when(f == 0)
    def _():
        xb_ref[...] = x_ref[...].astype(BF16)
        acc_ref[...] = jnp.zeros_like(acc_ref)

    xb = xb_ref[...]
    gate = jnp.dot(xb, wg_ref[...], preferred_element_type=F32)
    up = jnp.dot(xb, wu_ref[...], preferred_element_type=F32)
    h = (gate * jax.nn.sigmoid(gate) * up).astype(BF16)
    acc_ref[...] += jnp.dot(h, wo_ref[...], preferred_element_type=F32)

    @pl.when(f == pl.num_programs(1) - 1)
    def _():
        y = ALPHA * x_ref[...] + acc_ref[...]
        o_ref[...] = _layer_norm_rows(y, g_ref[...], b_ref[...])


def _ffn(x, w_in, w_out, g, b, *, tm=512, tf=512, name):
    T, D = x.shape
    dff = w_out.shape[0]
    tm = min(tm, T)
    assert T % tm == 0 and dff % tf == 0
    nf = dff // tf
    return pl.pallas_call(
        _ffn_kernel,
        grid=(T // tm, nf),
        in_specs=[pl.BlockSpec((tm, D), lambda i, f: (i, 0)),
                  pl.BlockSpec((D, tf), lambda i, f: (0, f)),
                  pl.BlockSpec((D, tf), lambda i, f: (0, f + nf)),
                  pl.BlockSpec((tf, D), lambda i, f: (f, 0)),
                  pl.BlockSpec((1, D), lambda i, f: (0, 0)),
                  pl.BlockSpec((1, D), lambda i, f: (0, 0))],
        out_specs=pl.BlockSpec((tm, D), lambda i, f: (i, 0)),
        out_shape=jax.ShapeDtypeStruct((T, D), F32),
        scratch_shapes=[pltpu.VMEM((tm, D), BF16), pltpu.VMEM((tm, D), F32)],
        compiler_params=_cparams(("parallel", "arbitrary")),
        name=name,
    )(x, w_in, w_in, w_out, g.reshape(1, D), b.reshape(1, D))


def _xattn_kernel(x_ref, wq_ref, k_ref, v_ref, o_ref):
    xb = x_ref[...].astype(BF16)
    q = jnp.dot(xb, wq_ref[...], preferred_element_type=F32).astype(BF16)
    scale = XA_HEAD_DIM ** -0.5
    for h in range(XA_HEADS):
        sl = slice(h * XA_HEAD_DIM, (h + 1) * XA_HEAD_DIM)
        s = lax.dot_general(q[:, sl], k_ref[:, sl], (((1,), (1,)), ((), ())),
                            preferred_element_type=F32) * scale
        m = jnp.max(s, axis=-1, keepdims=True)
        p = jnp.exp(s - m)
        l = jnp.sum(p, axis=-1, keepdims=True)
        p = (p / l).astype(BF16)
        o = jnp.dot(p, v_ref[:, sl], preferred_element_type=F32)
        o_ref[:, sl] = o.astype(o_ref.dtype)


def _xattn(x, wq, kv, batch, *, tm=512, name):
    T, D = x.shape
    S = T // batch
    M = kv.shape[0] // batch
    tm = min(tm, S)
    ns = S // tm
    return pl.pallas_call(
        _xattn_kernel,
        grid=(batch, ns),
        in_specs=[pl.BlockSpec((tm, D), lambda b, i: (b * ns + i, 0)),
                  pl.BlockSpec((D, D), lambda b, i: (0, 0)),
                  pl.BlockSpec((M, D), lambda b, i: (b, 0)),
                  pl.BlockSpec((M, D), lambda b, i: (b, 1))],
        out_specs=pl.BlockSpec((tm, D), lambda b, i: (b * ns + i, 0)),
        out_shape=jax.ShapeDtypeStruct((T, D), BF16),
        compiler_params=_cparams(("parallel", "arbitrary")),
        name=name,
    )(x, wq, kv, kv)


NT_DIMS = (((1,), (1,)), ((), ()))
LOG2E = math.log2(math.e)


def _causal_pairs(nq, ratio):
    qi = [i for i in range(nq) for _ in range((i + 1) * ratio)]
    kj = [j for i in range(nq) for j in range((i + 1) * ratio)]
    return jnp.asarray(qi, jnp.int32), jnp.asarray(kj, jnp.int32)


def _flash_update(q, k, v_ext, m_ref, acc_ref, idx, rows, bias=None, mask=None):
    s = lax.dot_general(q, k, NT_DIMS, preferred_element_type=F32)
    if bias is not None:
        s = s + bias
    if mask is not None:
        s = jnp.where(mask, s, NEG)
    chunks = [s[:, j * LANES:(j + 1) * LANES] for j in range(s.shape[1] // LANES)]
    mc = chunks[0]
    for ch in chunks[1:]:
        mc = jnp.maximum(mc, ch)
    m_prev = m_ref[idx, rows, :]
    m_new = jnp.maximum(m_prev, jnp.max(mc, axis=1, keepdims=True))
    p = jnp.concatenate([jnp.exp2(ch - m_new).astype(BF16) for ch in chunks], axis=1)
    alpha = jnp.exp2(m_prev - m_new)
    pv = jnp.dot(p, v_ext, preferred_element_type=F32)
    acc_ref[idx, rows, :] = jnp.concatenate([alpha, alpha], axis=1) * acc_ref[idx, rows, :] + pv
    m_ref[idx, rows, :] = m_new


def _flash_init(m_ref, acc_ref):
    m_ref[...] = jnp.full_like(m_ref, -jnp.inf)
    acc_ref[...] = jnp.zeros_like(acc_ref)


def _with_ones(v):
    return jnp.concatenate([v, jnp.ones_like(v)], axis=1)


def _diff_attn_kernel(lam_ref, q_ref, k_ref, v_ref, g_ref, o_ref, m_ref, acc_ref, *, tq, tk, rq, lam_init):
    qi = pl.program_id(2)
    ratio = tq // tk
    _flash_init(m_ref, acc_ref)

    def step(kj, d):
        k0 = pl.multiple_of(kj * tk, tk)
        k = k_ref[pl.ds(k0, tk), :]
        v_ext = _with_ones(v_ref[pl.ds(k0, tk), :])
        lane = lax.broadcasted_iota(jnp.int32, (rq, LANES), 1)
        for rb in range(tq // rq):
            if d is not None and (rb + 1) * rq <= d * tk:
                continue
            rows = slice(rb * rq, (rb + 1) * rq)
            q = q_ref[rows, :]
            mask = None
            if d is not None and rb * rq < (d + 1) * tk - 1:
                row = rb * rq + lax.broadcasted_iota(jnp.int32, (rq, tk), 0)
                col = d * tk + lax.broadcasted_iota(jnp.int32, (rq, tk), 1)
                mask = col <= row
            for c, sel in enumerate((lane < DIFF_HEAD_DIM, lane >= DIFF_HEAD_DIM)):
                _flash_update(jnp.where(sel, q, jnp.zeros_like(q)), k, v_ext, m_ref, acc_ref, c, rows, mask=mask)

    def below_diagonal(kj, carry):
        step(kj, None)
        return carry

    lax.fori_loop(0, qi * ratio, below_diagonal, 0)
    for d in range(ratio):
        step(qi * ratio + d, d)

    lam = lam_ref[0]
    a = (acc_ref[0, :, :LANES] / acc_ref[0, :, LANES:]
         - lam * (acc_ref[1, :, :LANES] / acc_ref[1, :, LANES:]))
    ms = jnp.mean(a * a, axis=-1, keepdims=True)
    a = a * lax.rsqrt(ms + LN_EPS) * g_ref[...]
    o_ref[...] = (a * (1.0 - lam_init)).astype(o_ref.dtype)


def _diff_attn(proj, lam, subln_g, batch, lam_init, *, tq=1024, tk=1024, rq=128, name):
    T = proj.shape[0]
    S = T // batch
    tq, tk = min(tq, S), min(tk, S)
    rq = min(rq, tq)
    assert S % tq == 0 and tq % tk == 0 and tq % rq == 0
    nq = S // tq
    H = DIFF_HEADS
    return pl.pallas_call(
        functools.partial(_diff_attn_kernel, tq=tq, tk=tk, rq=rq, lam_init=lam_init),
        grid=(batch, H, nq),
        in_specs=[pl.BlockSpec(memory_space=pltpu.SMEM),
                  pl.BlockSpec((tq, LANES), lambda b, h, i: (b * nq + i, h)),
                  pl.BlockSpec((S, LANES), lambda b, h, i: (b, H + h)),
                  pl.BlockSpec((S, LANES), lambda b, h, i: (b, 2 * H + h)),
                  pl.BlockSpec((1, LANES), lambda b, h, i: (0, 0))],
        out_specs=pl.BlockSpec((tq, LANES), lambda b, h, i: (b * nq + i, h)),
        out_shape=jax.ShapeDtypeStruct((T, DIFF_WIDTH), BF16),
        scratch_shapes=[pltpu.VMEM((2, tq, LANES), F32), pltpu.VMEM((2, tq, 2 * LANES), F32)],
        compiler_params=_cparams(("parallel", "parallel", "arbitrary")),
        name=name,
    )(lam.reshape(1).astype(F32), proj, proj, proj, subln_g.reshape(1, LANES).astype(F32))


CONV_HALO = 32
CONV_ROWS = 64


def _conv_kernel(val_ref, gate_ref, hval_ref, hgate_ref, cw_ref, cb_ref, g_ref, b_ref, o_ref,
                 u_ref, c_ref, *, tr):
    i = pl.program_id(1)
    rows = CONV_HALO + tr
    n_ch = CONV_CH // LANES
    halo_on = jnp.where(i > 0, 1.0, 0.0)
    for c in range(n_ch):
        cs = slice(c * LANES, (c + 1) * LANES)
        u_ref[0, c, CONV_HALO:rows, :] = val_ref[:, cs].astype(F32) * jax.nn.sigmoid(gate_ref[:, cs].astype(F32))
        halo = hval_ref[:, cs].astype(F32) * jax.nn.sigmoid(hgate_ref[:, cs].astype(F32))
        u_ref[0, c, :CONV_HALO, :] = halo * halo_on
        for e in range(1, SUBLANES):
            u_ref[e, c, :rows - SUBLANES, :] = u_ref[0, c, e:rows - SUBLANES + e, :]
    base = CONV_HALO - (CONV_WIDTH - 1)
    n_rb = tr // CONV_ROWS
    for c in range(n_ch):
        cs = slice(c * LANES, (c + 1) * LANES)
        accs = [jnp.broadcast_to(cb_ref[:, cs], (CONV_ROWS, LANES))] * n_rb
        for w in range(CONV_WIDTH):
            e = (base + w) % SUBLANES
            tap = jnp.broadcast_to(cw_ref[w:w + 1, cs], (CONV_ROWS, LANES))
            for r in range(n_rb):
                r0 = r * CONV_ROWS + base + w - e
                accs[r] = accs[r] + u_ref[e, c, r0:r0 + CONV_ROWS, :] * tap
        for r in range(n_rb):
            c_ref[c, r * CONV_ROWS:(r + 1) * CONV_ROWS, :] = accs[r]
    conv = jnp.concatenate([c_ref[c] for c in range(n_ch)], axis=1)
    y = _layer_norm_rows(conv, g_ref[...], b_ref[...])
    o_ref[...] = (y * jax.nn.sigmoid(y)).astype(o_ref.dtype)


def _conv_module(proj, conv_w, conv_b, ln_g, ln_b, batch, *, tr=256, name):
    T = proj.shape[0]
    S = T // batch
    tr = min(tr, S)
    ns = S // tr
    vb = (3 * DIFF_WIDTH) // CONV_CH
    hpb = tr // CONV_HALO

    def halo_idx(col):
        return lambda b, i: (jnp.maximum((b * ns + i) * hpb - 1, 0), col)

    vec = lambda a: a.reshape(1, CONV_CH).astype(F32)
    return pl.pallas_call(
        functools.partial(_conv_kernel, tr=tr),
        grid=(batch, ns),
        in_specs=[pl.BlockSpec((tr, CONV_CH), lambda b, i: (b * ns + i, vb)),
                  pl.BlockSpec((tr, CONV_CH), lambda b, i: (b * ns + i, vb + 1)),
                  pl.BlockSpec((CONV_HALO, CONV_CH), halo_idx(vb)),
                  pl.BlockSpec((CONV_HALO, CONV_CH), halo_idx(vb + 1)),
                  pl.BlockSpec((CONV_WIDTH, CONV_CH), lambda b, i: (0, 0)),
                  pl.BlockSpec((1, CONV_CH), lambda b, i: (0, 0)),
                  pl.BlockSpec((1, CONV_CH), lambda b, i: (0, 0)),
                  pl.BlockSpec((1, CONV_CH), lambda b, i: (0, 0))],
        out_specs=pl.BlockSpec((tr, CONV_CH), lambda b, i: (b * ns + i, 0)),
        out_shape=jax.ShapeDtypeStruct((T, CONV_CH), BF16),
        scratch_shapes=[pltpu.VMEM((SUBLANES, CONV_CH // LANES, CONV_HALO + tr, LANES), F32),
                        pltpu.VMEM((CONV_CH // LANES, tr, LANES), F32)],
        compiler_params=_cparams(("parallel", "arbitrary")),
        name=name,
    )(proj, proj, proj, proj, conv_w.astype(F32), vec(conv_b), vec(ln_g), vec(ln_b))


IDX_CHUNK = 1024
SLABS = IDX_CHUNK // LANES
KEY16_LO = -32640
KEY16_HI = 0x7F80
KEY16_BITS = 16


def _key_to_f32(key):
    bits = jnp.where(key < 0, key ^ jnp.int32(0x7FFFFFFF), key)
    return pltpu.bitcast(bits, F32)


def _indexer_kernel(q_ref, k_ref, o_ref, sc_ref, g_ref, *, tq, topk, seq):
    i = pl.program_id(1)
    q0 = i * tq
    nch = (q0 + tq + IDX_CHUNK - 1) // IDX_CHUNK
    n_all = seq // IDX_CHUNK
    lane = lax.broadcasted_iota(jnp.int32, (tq, LANES), 1)

    qh = []
    for blk in range(IDX_Q // LANES):
        qb = q_ref[:, blk * LANES:(blk + 1) * LANES]
        qh.append(jnp.where(lane < IDX_DIM, qb, 0.0).astype(BF16))
        qh.append(jnp.where(lane < IDX_DIM, pltpu.roll(qb, IDX_DIM, 1), 0.0).astype(BF16))
    wcol = IDX_Q + IDX_DIM
    wscale = IDX_HEADS ** -0.5 * IDX_DIM ** -0.5
    wts = [q_ref[:, wcol + h:wcol + h + 1] * wscale for h in range(IDX_HEADS)]

    def score_chunk(c, masked):
        k0 = pl.multiple_of(c * IDX_CHUNK, IDX_CHUNK)
        kc = k_ref[pl.ds(k0, IDX_CHUNK), :]
        sc = None
        for h in range(IDX_HEADS):
            r = lax.dot_general(qh[h], kc, (((1,), (1,)), ((), ())), preferred_element_type=F32)
            t = wts[h] * jnp.maximum(r, 0.0)
            sc = t if sc is None else sc + t
        if masked:
            row = q0 + lax.broadcasted_iota(jnp.int32, (tq, IDX_CHUNK), 0)
            col = k0 + lax.broadcasted_iota(jnp.int32, (tq, IDX_CHUNK), 1)
            sc = jnp.where(col <= row, sc, -jnp.inf)
        hi_bits = pltpu.bitcast(sc, jnp.int32) & jnp.int32(-(1 << KEY16_BITS))
        g = pltpu.bitcast(hi_bits, F32).astype(BF16)
        for j in range(SLABS):
            sc_ref[c * SLABS + j] = sc[:, j * LANES:(j + 1) * LANES]
            g_ref[c * SLABS + j] = g[:, j * LANES:(j + 1) * LANES]

    def load_scores(c):
        return jnp.concatenate([sc_ref[c * SLABS + j] for j in range(SLABS)], axis=1)

    def unmasked_body(c, carry):
        score_chunk(c, False)
        return carry

    lax.fori_loop(0, nch - 1, unmasked_body, 0)
    score_chunk(nch - 1, True)

    def count_in(ref, dtype, pred):
        one = jnp.ones((tq, LANES), dtype)
        zero = jnp.zeros((tq, LANES), dtype)

        def body(c, acc):
            for j in range(SLABS):
                acc = acc + jnp.where(pred(ref[c * SLABS + j]), one, zero)
            return acc
        acc = lax.fori_loop(0, nch, body, zero)
        return jnp.broadcast_to(jnp.sum(acc.astype(F32), axis=1, keepdims=True), (tq, LANES))

    def count(pred):
        return count_in(sc_ref, F32, pred)

    kf = float(topk)
    def bisect16(_, carry):
        lo, hi, cnt_lo = carry
        mid = (lo + hi) >> 1
        key = jnp.where(mid < 0, (mid << KEY16_BITS) | jnp.int32(0xFFFF), mid << KEY16_BITS)
        cand = _key_to_f32(key).astype(BF16)
        cnt = count_in(g_ref, BF16, lambda blk: blk >= cand)
        ok = cnt >= kf
        return jnp.where(ok, mid, lo), jnp.where(ok, hi, mid), jnp.where(ok, cnt, cnt_lo)

    lo16 = jnp.full((tq, LANES), KEY16_LO, jnp.int32)
    hi16 = jnp.full((tq, LANES), KEY16_HI, jnp.int32)
    lo16, _, cnt_lo = lax.fori_loop(0, KEY16_BITS, bisect16, (lo16, hi16, jnp.zeros((tq, LANES), F32)))

    def unresolved(carry):
        lo, hi, _ = carry
        return jnp.max(hi - lo) > 1

    def bisect32(carry):
        lo, hi, cnt_lo = carry
        mid = lo + ((hi - lo) >> 1)
        cand = _key_to_f32(mid)
        cnt = count(lambda blk: blk >= cand)
        ok = cnt >= kf
        hi = jnp.where(cnt == kf, mid + 1, jnp.where(ok, hi, mid))
        return jnp.where(ok, mid, lo), hi, jnp.where(ok, cnt, cnt_lo)

    lo32 = lo16 << KEY16_BITS
    at_start = count(lambda blk: blk > _key_to_f32(lo32)) < kf
    hi32 = jnp.where(at_start, lo32 + 1, (lo16 + 1) << KEY16_BITS)
    lo, _, n_ge = lax.while_loop(unresolved, bisect32, (lo32, hi32, cnt_lo))
    thr = _key_to_f32(lo)
    has_tie = jnp.max(n_ge) > kf
    thr_c = thr[:, :1]

    @pl.when(jnp.logical_not(has_tie))
    def _():
        def body(c, carry):
            k0 = pl.multiple_of(c * IDX_CHUNK, IDX_CHUNK)
            sc = load_scores(c)
            o_ref[0, :, pl.ds(k0, IDX_CHUNK)] = jnp.where(sc >= thr_c, 0.0, NEG).astype(o_ref.dtype)
            return carry
        lax.fori_loop(0, nch, body, 0)

    @pl.when(has_tie)
    def _():
        r_i = lax.broadcasted_iota(jnp.int32, (IDX_CHUNK, IDX_CHUNK), 0)
        c_i = lax.broadcasted_iota(jnp.int32, (IDX_CHUNK, IDX_CHUNK), 1)
        upper = jnp.where(r_i < c_i, 1.0, 0.0).astype(BF16)
        need_c = (kf - count(lambda blk: blk > thr))[:, :1]
        tie_c = jnp.where(n_ge > kf, 1.0, 0.0)[:, :1]

        def pending(carry):
            c, seen = carry
            return (c < nch) & (jnp.max(tie_c * (need_c - seen)) > 0.0)

        def body(carry):
            c, seen = carry
            k0 = pl.multiple_of(c * IDX_CHUNK, IDX_CHUNK)
            sc = load_scores(c)
            eq = jnp.where(sc == thr_c, 1.0, 0.0)
            before = seen + jnp.dot(eq.astype(BF16), upper, preferred_element_type=F32)
            keep = (sc > thr_c) | ((sc == thr_c) & (before < need_c))
            o_ref[0, :, pl.ds(k0, IDX_CHUNK)] = jnp.where(keep, 0.0, NEG).astype(o_ref.dtype)
            return c + 1, seen + jnp.sum(eq, axis=1, keepdims=True)
        c_done, _ = lax.while_loop(pending, body, (jnp.int32(0), jnp.zeros((tq, 1), F32)))

        def rest(c, carry):
            k0 = pl.multiple_of(c * IDX_CHUNK, IDX_CHUNK)
            sc = load_scores(c)
            keep = (sc > thr_c) | ((sc == thr_c) & (tie_c < 0.5))
            o_ref[0, :, pl.ds(k0, IDX_CHUNK)] = jnp.where(keep, 0.0, NEG).astype(o_ref.dtype)
            return carry
        lax.fori_loop(c_done, nch, rest, 0)

    def fill(c, carry):
        k0 = pl.multiple_of(c * IDX_CHUNK, IDX_CHUNK)
        o_ref[0, :, pl.ds(k0, IDX_CHUNK)] = jnp.full((tq, IDX_CHUNK), NEG, o_ref.dtype)
        return carry
    lax.fori_loop(nch, n_all, fill, 0)


def _indexer(idx, kidx, batch, *, tq=128, name):
    T = idx.shape[0]
    S = T // batch
    tq = min(tq, S)
    ns = S // tq
    topk = min(TOPK_MAX, S // 4)
    assert S % IDX_CHUNK == 0 and IDX_CHUNK % tq == 0
    return pl.pallas_call(
        functools.partial(_indexer_kernel, tq=tq, topk=topk, seq=S),
        grid=(batch, ns),
        in_specs=[pl.BlockSpec((tq, IDX_PAD), lambda b, i: (b * ns + i, 0)),
                  pl.BlockSpec((S, LANES), lambda b, i: (b, 0))],
        out_specs=pl.BlockSpec((1, tq, S), lambda b, i: (b, i, 0)),
        out_shape=jax.ShapeDtypeStruct((batch, S, S), BF16),
        scratch_shapes=[pltpu.VMEM((S // LANES, tq, LANES), F32), pltpu.VMEM((S // LANES, tq, LANES), BF16)],
        compiler_params=_cparams(("parallel", "arbitrary")),
        name=name,
    )(idx, kidx)


def _dsa_attn_kernel(qi_ref, kj_ref, q_ref, k_ref, v_ref, bias_ref, o_ref, m_ref, acc_ref, *, tq, tk, rq):
    p_id = pl.program_id(2)
    qi = qi_ref[p_id]
    kj = kj_ref[p_id]
    ratio = tq // tk

    @pl.when(kj == 0)
    def _():
        _flash_init(m_ref, acc_ref)

    def step(d):
        k = k_ref[...]
        v_ext = _with_ones(v_ref[...])
        for rb in range(tq // rq):
            if d is not None and (rb + 1) * rq <= d * tk:
                continue
            rows = slice(rb * rq, (rb + 1) * rq)
            bias = bias_ref[0, rows, :].astype(F32)
            for r in range(DSA_REP):
                q = q_ref[rows, r * DSA_HEAD_DIM:(r + 1) * DSA_HEAD_DIM]
                _flash_update(q, k, v_ext, m_ref, acc_ref, r, rows, bias=bias)

    @pl.when(kj < qi * ratio)
    def _():
        step(None)

    for d in range(ratio):
        @pl.when(kj == qi * ratio + d)
        def _(d=d):
            step(d)

    @pl.when(kj == (qi + 1) * ratio - 1)
    def _():
        for r in range(DSA_REP):
            o = acc_ref[r, :, :DSA_HEAD_DIM] / acc_ref[r, :, DSA_HEAD_DIM:]
            o_ref[:, r * DSA_HEAD_DIM:(r + 1) * DSA_HEAD_DIM] = o.astype(o_ref.dtype)


def _dsa_attn(qkv, bias, batch, *, tq=1024, tk=1024, rq=128, name):
    T = qkv.shape[0]
    S = T // batch
    tq, tk = min(tq, S), min(tk, S)
    rq = min(rq, tq)
    assert S % tq == 0 and tq % tk == 0 and tq % rq == 0
    nq, nk = S // tq, S // tk
    qi, kj = _causal_pairs(nq, tq // tk)
    G = DSA_KV_HEADS
    gw = DSA_REP * DSA_HEAD_DIM
    kb = DSA_Q // DSA_HEAD_DIM
    grid_spec = pltpu.PrefetchScalarGridSpec(
        num_scalar_prefetch=2,
        grid=(batch, G, int(qi.shape[0])),
        in_specs=[pl.BlockSpec((tq, gw), lambda b, g, p, qi, kj: (b * nq + qi[p], g)),
                  pl.BlockSpec((tk, DSA_HEAD_DIM), lambda b, g, p, qi, kj: (b * nk + kj[p], kb + g)),
                  pl.BlockSpec((tk, DSA_HEAD_DIM), lambda b, g, p, qi, kj: (b * nk + kj[p], kb + G + g)),
                  pl.BlockSpec((1, tq, tk), lambda b, g, p, qi, kj: (b, qi[p], kj[p]))],
        out_specs=pl.BlockSpec((tq, gw), lambda b, g, p, qi, kj: (b * nq + qi[p], g)),
        scratch_shapes=[pltpu.VMEM((DSA_REP, tq, DSA_HEAD_DIM), F32),
                        pltpu.VMEM((DSA_REP, tq, 2 * DSA_HEAD_DIM), F32)])
    return pl.pallas_call(
        functools.partial(_dsa_attn_kernel, tq=tq, tk=tk, rq=rq),
        grid_spec=grid_spec,
        out_shape=jax.ShapeDtypeStruct((T, DSA_Q), BF16),
        compiler_params=_cparams(("parallel", "parallel", "arbitrary")),
        name=name,
    )(qi, kj, qkv, qkv, qkv, bias)


def _even_mixer(x, positions, batch, w_in, w_out, lam_p, subln_g, conv_w, conv_b, conv_ln_g, conv_ln_b,
                lam_init, ln_g, ln_b):
    tabs, half = _rot_tables(positions, DIFF_HEAD_DIM)
    n_rot = (2 * DIFF_WIDTH) // LANES
    chunk_types = [0] * n_rot + [-1] * ((EVEN_IN - 2 * DIFF_WIDTH) // LANES)
    q_scale = DIFF_HEAD_DIM ** -0.5 * LOG2E
    chunk_scales = [q_scale] * (DIFF_WIDTH // LANES) + [1.0] * ((EVEN_IN - DIFF_WIDTH) // LANES)
    proj = _inproj(x, w_in.astype(BF16), tabs, (half,), chunk_types, BF16, tm=1024, tn=1024, name="even_inproj",
                   chunk_scales=chunk_scales)
    lp = lam_p.astype(F32)
    lam = jnp.exp(jnp.sum(lp[0] * lp[1])) - jnp.exp(jnp.sum(lp[2] * lp[3])) + lam_init
    a = _diff_attn(proj, lam, subln_g, batch, lam_init, name="diff_attn")
    c = _conv_module(proj, conv_w, conv_b, conv_ln_g, conv_ln_b, batch, name="conv_module")
    return _proj_ln((a, c), w_out.astype(BF16), x, ln_g, ln_b, name="even_outproj_ln")


def _odd_mixer(x, positions, batch, w_in, w_out, ln_g, ln_b):
    T = x.shape[0]
    tabs_b, half_b = _rot_tables(positions, DSA_HEAD_DIM)
    n_rot = (DSA_Q + DSA_KV) // LANES
    chunk_types = [0] * n_rot + [-1] * (DSA_KV // LANES)
    w_main = w_in[:, :ODD_MAIN].astype(BF16)
    q_scale = DSA_HEAD_DIM ** -0.5 * LOG2E
    chunk_scales = [q_scale] * (DSA_Q // LANES) + [1.0] * ((ODD_MAIN - DSA_Q) // LANES)
    qkv = _inproj(x, w_main, tabs_b, (half_b,), chunk_types, BF16, tm=1024, tn=1024, name="odd_inproj",
                  chunk_scales=chunk_scales)
    w_idx = jnp.pad(w_in[:, ODD_MAIN:], ((0, 0), (0, IDX_PAD - (w_in.shape[1] - ODD_MAIN)))).astype(BF16)
    tabs_a, half_a = _rot_tables(positions, IDX_DIM)
    tabs_ah, _ = _rot_tables(positions, IDX_DIM, active_lanes=IDX_DIM)
    idx_types = [0] * (IDX_Q // LANES) + [1] + [-1] * ((IDX_PAD - IDX_Q) // LANES - 1)
    idx = _inproj(x, w_idx, jnp.concatenate([tabs_a, tabs_ah]), (half_a, half_a), idx_types, F32,
                  tm=1024, tn=IDX_PAD, name="idx_inproj")
    kidx = idx[:, IDX_Q:IDX_Q + LANES].astype(BF16)
    bias = _indexer(idx, kidx, batch, name="indexer")
    o = _dsa_attn(qkv, bias, batch, name="dsa_attn")
    return _proj_ln((o,), w_out.astype(BF16), x, ln_g, ln_b, name="odd_outproj_ln")


def kernel(x, mem, positions, w_in_even, w_out_even, diff_lambda, diff_subln_g, conv_w, conv_b, conv_ln_g,
           conv_ln_b, w_in_odd, w_out_odd, xa_wq, xa_wkv, xa_wo, ffn_w_in, ffn_w_out, ln_g, ln_b):
    B, S, D = x.shape
    M = mem.shape[1]
    x = x.reshape(B * S, D)
    mem2 = mem.reshape(B * M, D)
    for layer in range(DEPTH):
        j = layer // 2
        if layer % 2 == 0:
            lam_init = 0.8 - 0.6 * math.exp(-0.3 * layer)
            x = _even_mixer(x, positions, B, w_in_even[j], w_out_even[j], diff_lambda[j], diff_subln_g[j],
                            conv_w[j], conv_b[j], conv_ln_g[j], conv_ln_b[j], lam_init,
                            ln_g[layer, 0], ln_b[layer, 0])
        else:
            x = _odd_mixer(x, positions, B, w_in_odd[j], w_out_odd[j], ln_g[layer, 0], ln_b[layer, 0])
        kv = _inproj(mem2, xa_wkv[layer].astype(BF16), None, (), [-1] * (2 * D // LANES), BF16,
                     tm=512, tn=512, name=f"xa_kvproj_{layer}")
        o = _xattn(x, xa_wq[layer].astype(BF16), kv, B, name=f"xattn_{layer}")
        x = _proj_ln((o,), xa_wo[layer].astype(BF16), x, ln_g[layer, 1], ln_b[layer, 1],
                     name=f"xa_outproj_ln_{layer}")
        x = _ffn(x, ffn_w_in[layer].astype(BF16), ffn_w_out[layer].astype(BF16),
                 ln_g[layer, 2], ln_b[layer, 2], name=f"ffn_{layer}")
    return x.reshape(B, S, D)
```

```python
import functools
import math

import jax
import jax.numpy as jnp
from jax import lax
from jax.experimental import pallas as pl
from jax.experimental.pallas import tpu as pltpu

F32 = jnp.float32
BF16 = jnp.bfloat16

D_MODEL = 2048
DEPTH = 2
ALPHA = (2 * DEPTH) ** 0.25
LN_EPS = 1e-5
ROPE_THETA = 500000.0
ROPE_FRAC = 4

DIFF_HEADS = 8
DIFF_HEAD_DIM = 64
DIFF_V_DIM = 2 * DIFF_HEAD_DIM
DIFF_WIDTH = DIFF_HEADS * DIFF_V_DIM
CONV_CH = D_MODEL - DIFF_WIDTH
CONV_WIDTH = 31
EVEN_IN = 3 * DIFF_WIDTH + 2 * CONV_CH

DSA_HEADS = 16
DSA_KV_HEADS = 4
DSA_HEAD_DIM = 128
DSA_REP = DSA_HEADS // DSA_KV_HEADS
IDX_HEADS = 4
IDX_DIM = 64
TOPK_MAX = 256
DSA_Q = DSA_HEADS * DSA_HEAD_DIM
DSA_KV = DSA_KV_HEADS * DSA_HEAD_DIM
IDX_Q = IDX_HEADS * IDX_DIM
ODD_MAIN = DSA_Q + 2 * DSA_KV
IDX_PAD = 512

XA_HEADS = 4
XA_HEAD_DIM = D_MODEL // XA_HEADS
D_FF = -(-(8 * D_MODEL) // (3 * 256)) * 256

LANES = 128
SUBLANES = 8
MXU_COLS = 256
NEG = -1e30
VMEM_LIMIT = 56 * 1024 * 1024


def _cparams(sem):
    return pltpu.CompilerParams(dimension_semantics=sem, vmem_limit_bytes=VMEM_LIMIT)


def _layer_norm_rows(y, g, b):
    mu = jnp.mean(y, axis=-1, keepdims=True)
    d = y - mu
    var = jnp.mean(d * d, axis=-1, keepdims=True)
    return d * lax.rsqrt(var + LN_EPS) * g + b


def _rot_tables(positions, head_dim, active_lanes=LANES):
    rot = head_dim // ROPE_FRAC
    half = rot // 2
    inv_freq = 1.0 / (ROPE_THETA ** (jnp.arange(half, dtype=F32) / half))
    ang = positions.reshape(-1, 1).astype(F32) * inv_freq
    cos, sin = jnp.cos(ang), jnp.sin(ang)
    lane = jnp.arange(LANES)
    d = lane % head_dim
    f = d % half
    live = lane < active_lanes
    cos_l = jnp.take(cos, f, axis=1)
    sin_l = jnp.take(sin, f, axis=1)
    c = jnp.where((d < rot) & live, cos_l, 1.0)
    s1 = jnp.where((d >= half) & (d < rot) & live, sin_l, 0.0)
    s2 = jnp.where((d < half) & live, -sin_l, 0.0)
    return jnp.stack([c, s1, s2]), half


def _inproj_kernel(x_ref, w_ref, tab_ref, o_ref, xb_ref, *, variants, shifts):
    j = pl.program_id(1)

    @pl.when(j == 0)
    def _():
        xb_ref[...] = x_ref[...].astype(BF16)

    acc = jnp.dot(xb_ref[...], w_ref[...], preferred_element_type=F32)

    for (j_lo, j_hi), types in variants:
        @pl.when((j >= j_lo) & (j < j_hi))
        def _(types=types):
            for c, (p, scale) in enumerate(types):
                blk = acc[:, c * LANES:(c + 1) * LANES]
                if p >= 0:
                    sh = shifts[p]
                    blk = (blk * tab_ref[3 * p]
                           + pltpu.roll(blk, sh, 1) * tab_ref[3 * p + 1]
                           + pltpu.roll(blk, LANES - sh, 1) * tab_ref[3 * p + 2])
                if scale != 1.0:
                    blk = blk * scale
                o_ref[:, c * LANES:(c + 1) * LANES] = blk.astype(o_ref.dtype)


def _inproj(x, w, tabs, shifts, chunk_types, out_dtype, *, tm, tn, name, chunk_scales=None):
    T, K = x.shape
    N = w.shape[1]
    tm = min(tm, T)
    assert T % tm == 0 and N % tn == 0 and len(chunk_types) == N // LANES
    if chunk_scales is None:
        chunk_scales = [1.0] * len(chunk_types)
    chunk_types = list(zip(chunk_types, chunk_scales))
    per = tn // LANES
    tile_types = [tuple(chunk_types[t * per:(t + 1) * per]) for t in range(N // tn)]
    variants = []
    for t, ty in enumerate(tile_types):
        if variants and variants[-1][1] == ty and variants[-1][0][1] == t:
            variants[-1] = ((variants[-1][0][0], t + 1), ty)
        else:
            variants.append(((t, t + 1), ty))
    if tabs is None:
        tabs = jnp.zeros((3, T, LANES), F32)
    P3 = tabs.shape[0]
    return pl.pallas_call(
        functools.partial(_inproj_kernel, variants=tuple(variants), shifts=tuple(shifts)),
        grid=(T // tm, N // tn),
        in_specs=[pl.BlockSpec((tm, K), lambda i, j: (i, 0)),
                  pl.BlockSpec((K, tn), lambda i, j: (0, j)),
                  pl.BlockSpec((P3, tm, LANES), lambda i, j: (0, i, 0))],
        out_specs=pl.BlockSpec((tm, tn), lambda i, j: (i, j)),
        out_shape=jax.ShapeDtypeStruct((T, N), out_dtype),
        scratch_shapes=[pltpu.VMEM((tm, K), BF16)],
        compiler_params=_cparams(("parallel", "arbitrary")),
        name=name,
    )(x, w, tabs)


def _proj_ln_kernel(*refs, widths):
    h_refs = refs[:len(widths)]
    w_ref, x_ref, g_ref, b_ref, o_ref = refs[len(widths):]
    y = ALPHA * x_ref[...]
    k0 = 0
    for h_ref, kw in zip(h_refs, widths):
        y = y + jnp.dot(h_ref[...], w_ref[k0:k0 + kw, :], preferred_element_type=F32)
        k0 += kw
    o_ref[...] = _layer_norm_rows(y, g_ref[...], b_ref[...])


def _proj_ln(hs, w, x, g, b, *, tm=512, name):
    T = x.shape[0]
    K, D = w.shape
    widths = tuple(h.shape[1] for h in hs)
    tm = min(tm, T)
    assert T % tm == 0 and sum(widths) == K
    return pl.pallas_call(
        functools.partial(_proj_ln_kernel, widths=widths),
        grid=(T // tm,),
        in_specs=[pl.BlockSpec((tm, kw), lambda i: (i, 0)) for kw in widths] + [
                  pl.BlockSpec((K, D), lambda i: (0, 0), pipeline_mode=pl.Buffered(1)),
                  pl.BlockSpec((tm, D), lambda i: (i, 0)),
                  pl.BlockSpec((1, D), lambda i: (0, 0)),
                  pl.BlockSpec((1, D), lambda i: (0, 0))],
        out_specs=pl.BlockSpec((tm, D), lambda i: (i, 0)),
        out_shape=jax.ShapeDtypeStruct((T, D), F32),
        compiler_params=_cparams(("parallel",)),
        name=name,
    )(*hs, w, x, g.reshape(1, D), b.reshape(1, D))


def _ffn_kernel(x_ref, wg_ref, wu_ref, wo_ref, g_ref, b_ref, o_ref, xb_ref, acc_ref):
    f = pl.program_id(1)

    @pl.when(f == 0)
    def _():
        xb_ref[...] = x_ref[...].astype(BF16)
        acc_ref[...] = jnp.zeros_like(acc_ref)

    xb = xb_ref[...]
    gate = jnp.dot(xb, wg_ref[...], preferred_element_type=F32)
    up = jnp.dot(xb, wu_ref[...], preferred_element_type=F32)
    h = (gate * jax.nn.sigmoid(gate) * up).astype(BF16)
    acc_ref[...] += jnp.dot(h, wo_ref[...], preferred_element_type=F32)

    @pl.when(f == pl.num_programs(1) - 1)
    def _():
        y = ALPHA * x_ref[...] + acc_ref[...]
        o_ref[...] = _layer_norm_rows(y, g_ref[...], b_ref[...])


def _ffn(x, w_in, w_out, g, b, *, tm=512, tf=512, name):
    T, D = x.shape
    dff = w_out.shape[0]
    tm = min(tm, T)
    assert T % tm == 0 and dff % tf == 0
    nf = dff // tf
    return pl.pallas_call(
        _ffn_kernel,
        grid=(T // tm, nf),
        in_specs=[pl.BlockSpec((tm, D), lambda i, f: (i, 0)),
                  pl.BlockSpec((D, tf), lambda i, f: (0, f)),
                  pl.BlockSpec((D, tf), lambda i, f: (0, f + nf)),
                  pl.BlockSpec((tf, D), lambda i, f: (f, 0)),
                  pl.BlockSpec((1, D), lambda i, f: (0, 0)),
                  pl.BlockSpec((1, D), lambda i, f: (0, 0))],
        out_specs=pl.BlockSpec((tm, D), lambda i, f: (i, 0)),
        out_shape=jax.ShapeDtypeStruct((T, D), F32),
        scratch_shapes=[pltpu.VMEM((tm, D), BF16), pltpu.VMEM((tm, D), F32)],
        compiler_params=_cparams(("parallel", "arbitrary")),
        name=name,
    )(x, w_in, w_in, w_out, g.reshape(1, D), b.reshape(1, D))


def _xattn_kernel(x_ref, wq_ref, k_ref, v_ref, o_ref):
    xb = x_ref[...].astype(BF16)
    q = jnp.dot(xb, wq_ref[...], preferred_element_type=F32).astype(BF16)
    scale = XA_HEAD_DIM ** -0.5
    for h in range(XA_HEADS):
        sl = slice(h * XA_HEAD_DIM, (h + 1) * XA_HEAD_DIM)
        s = lax.dot_general(q[:, sl], k_ref[:, sl], (((1,), (1,)), ((), ())),
                            preferred_element_type=F32) * scale
        m = jnp.max(s, axis=-1, keepdims=True)
        p = jnp.exp(s - m)
        l = jnp.sum(p, axis=-1, keepdims=True)
        p = (p / l).astype(BF16)
        o = jnp.dot(p, v_ref[:, sl], preferred_element_type=F32)
        o_ref[:, sl] = o.astype(o_ref.dtype)


def _xattn(x, wq, kv, batch, *, tm=512, name):
    T, D = x.shape
    S = T // batch
    M = kv.shape[0] // batch
    tm = min(tm, S)
    ns = S // tm
    return pl.pallas_call(
        _xattn_kernel,
        grid=(batch, ns),
        in_specs=[pl.BlockSpec((tm, D), lambda b, i: (b * ns + i, 0)),
                  pl.BlockSpec((D, D), lambda b, i: (0, 0)),
                  pl.BlockSpec((M, D), lambda b, i: (b, 0)),
                  pl.BlockSpec((M, D), lambda b, i: (b, 1))],
        out_specs=pl.BlockSpec((tm, D), lambda b, i: (b * ns + i, 0)),
        out_shape=jax.ShapeDtypeStruct((T, D), BF16),
        compiler_params=_cparams(("parallel", "arbitrary")),
        name=name,
    )(x, wq, kv, kv)


NT_DIMS = (((1,), (1,)), ((), ()))
LOG2E = math.log2(math.e)


def _causal_pairs(nq, ratio):
    qi = [i for i in range(nq) for _ in range((i + 1) * ratio)]
    kj = [j for i in range(nq) for j in range((i + 1) * ratio)]
    return jnp.asarray(qi, jnp.int32), jnp.asarray(kj, jnp.int32)


def _flash_update(q, k, v_ext, m_ref, acc_ref, idx, rows, bias=None, mask=None):
    s = lax.dot_general(q, k, NT_DIMS, preferred_element_type=F32)
    if bias is not None:
        s = s + bias
    if mask is not None:
        s = jnp.where(mask, s, NEG)
    chunks = [s[:, j * LANES:(j + 1) * LANES] for j in range(s.shape[1] // LANES)]
    mc = chunks[0]
    for ch in chunks[1:]:
        mc = jnp.maximum(mc, ch)
    m_prev = m_ref[idx, rows, :]
    m_new = jnp.maximum(m_prev, jnp.max(mc, axis=1, keepdims=True))
    p = jnp.concatenate([jnp.exp2(ch - m_new).astype(BF16) for ch in chunks], axis=1)
    alpha = jnp.exp2(m_prev - m_new)
    pv = jnp.dot(p, v_ext, preferred_element_type=F32)
    acc_ref[idx, rows, :] = jnp.concatenate([alpha, alpha], axis=1) * acc_ref[idx, rows, :] + pv
    m_ref[idx, rows, :] = m_new


def _causal_width(rb, rq, d, tk):
    seen = (rb + 1) * rq - d * tk
    return min(tk, -(-seen // MXU_COLS) * MXU_COLS)


def _flash_init(m_ref, acc_ref):
    m_ref[...] = jnp.full_like(m_ref, -jnp.inf)
    acc_ref[...] = jnp.zeros_like(acc_ref)


def _with_ones(v):
    return jnp.concatenate([v, jnp.ones_like(v)], axis=1)


def _diff_attn_kernel(lam_ref, q_ref, k_ref, v_ref, g_ref, o_ref, m_ref, acc_ref, *, tq, tk, rq, lam_init):
    qi = pl.program_id(2)
    ratio = tq // tk
    _flash_init(m_ref, acc_ref)

    def step(kj, d):
        k0 = pl.multiple_of(kj * tk, tk)
        k = k_ref[pl.ds(k0, tk), :]
        v_ext = _with_ones(v_ref[pl.ds(k0, tk), :])
        lane = lax.broadcasted_iota(jnp.int32, (rq, LANES), 1)
        for rb in range(tq // rq):
            if d is not None and (rb + 1) * rq <= d * tk:
                continue
            rows = slice(rb * rq, (rb + 1) * rq)
            q = q_ref[rows, :]
            mask = None
            nk = tk if d is None else _causal_width(rb, rq, d, tk)
            if d is not None and rb * rq < d * tk + nk - 1:
                row = rb * rq + lax.broadcasted_iota(jnp.int32, (rq, nk), 0)
                col = d * tk + lax.broadcasted_iota(jnp.int32, (rq, nk), 1)
                mask = col <= row
            for c, sel in enumerate((lane < DIFF_HEAD_DIM, lane >= DIFF_HEAD_DIM)):
                _flash_update(jnp.where(sel, q, jnp.zeros_like(q)), k[:nk], v_ext[:nk], m_ref, acc_ref, c, rows,
                              mask=mask)

    def below_diagonal(kj, carry):
        step(kj, None)
        return carry

    lax.fori_loop(0, qi * ratio, below_diagonal, 0)
    for d in range(ratio):
        step(qi * ratio + d, d)

    lam = lam_ref[0]
    a = (acc_ref[0, :, :LANES] / acc_ref[0, :, LANES:]
         - lam * (acc_ref[1, :, :LANES] / acc_ref[1, :, LANES:]))
    ms = jnp.mean(a * a, axis=-1, keepdims=True)
    a = a * lax.rsqrt(ms + LN_EPS) * g_ref[...]
    o_ref[...] = (a * (1.0 - lam_init)).astype(o_ref.dtype)


def _diff_attn(proj, lam, subln_g, batch, lam_init, *, tq=2048, tk=1024, rq=128, name):
    T = proj.shape[0]
    S = T // batch
    tq, tk = min(tq, S), min(tk, S)
    rq = min(rq, tq)
    assert S % tq == 0 and tq % tk == 0 and tq % rq == 0
    nq = S // tq
    H = DIFF_HEADS
    return pl.pallas_call(
        functools.partial(_diff_attn_kernel, tq=tq, tk=tk, rq=rq, lam_init=lam_init),
        grid=(batch, H, nq),
        in_specs=[pl.BlockSpec(memory_space=pltpu.SMEM),
                  pl.BlockSpec((tq, LANES), lambda b, h, i: (b * nq + i, h)),
                  pl.BlockSpec((S, LANES), lambda b, h, i: (b, H + h)),
                  pl.BlockSpec((S, LANES), lambda b, h, i: (b, 2 * H + h)),
                  pl.BlockSpec((1, LANES), lambda b, h, i: (0, 0))],
        out_specs=pl.BlockSpec((tq, LANES), lambda b, h, i: (b * nq + i, h)),
        out_shape=jax.ShapeDtypeStruct((T, DIFF_WIDTH), BF16),
        scratch_shapes=[pltpu.VMEM((2, tq, LANES), F32), pltpu.VMEM((2, tq, 2 * LANES), F32)],
        compiler_params=_cparams(("parallel", "parallel", "arbitrary")),
        name=name,
    )(lam.reshape(1).astype(F32), proj, proj, proj, subln_g.reshape(1, LANES).astype(F32))


CONV_HALO = 32
CONV_ROWS = 64


def _conv_kernel(val_ref, gate_ref, hval_ref, hgate_ref, cw_ref, cb_ref, g_ref, b_ref, o_ref,
                 u_ref, c_ref, *, tr):
    i = pl.program_id(1)
    rows = CONV_HALO + tr
    n_ch = CONV_CH // LANES
    halo_on = jnp.where(i > 0, 1.0, 0.0)
    for c in range(n_ch):
        cs = slice(c * LANES, (c + 1) * LANES)
        u_ref[0, c, CONV_HALO:rows, :] = val_ref[:, cs].astype(F32) * jax.nn.sigmoid(gate_ref[:, cs].astype(F32))
        halo = hval_ref[:, cs].astype(F32) * jax.nn.sigmoid(hgate_ref[:, cs].astype(F32))
        u_ref[0, c, :CONV_HALO, :] = halo * halo_on
        for e in range(1, SUBLANES):
            u_ref[e, c, :rows - SUBLANES, :] = u_ref[0, c, e:rows - SUBLANES + e, :]
    base = CONV_HALO - (CONV_WIDTH - 1)
    n_rb = tr // CONV_ROWS
    for c in range(n_ch):
        cs = slice(c * LANES, (c + 1) * LANES)
        accs = [jnp.broadcast_to(cb_ref[:, cs], (CONV_ROWS, LANES))] * n_rb
        for w in range(CONV_WIDTH):
            e = (base + w) % SUBLANES
            tap = jnp.broadcast_to(cw_ref[w:w + 1, cs], (CONV_ROWS, LANES))
            for r in range(n_rb):
                r0 = r * CONV_ROWS + base + w - e
                accs[r] = accs[r] + u_ref[e, c, r0:r0 + CONV_ROWS, :] * tap
        for r in range(n_rb):
            c_ref[c, r * CONV_ROWS:(r + 1) * CONV_ROWS, :] = accs[r]
    conv = jnp.concatenate([c_ref[c] for c in range(n_ch)], axis=1)
    y = _layer_norm_rows(conv, g_ref[...], b_ref[...])
    o_ref[...] = (y * jax.nn.sigmoid(y)).astype(o_ref.dtype)


def _conv_module(proj, conv_w, conv_b, ln_g, ln_b, batch, *, tr=256, name):
    T = proj.shape[0]
    S = T // batch
    tr = min(tr, S)
    ns = S // tr
    vb = (3 * DIFF_WIDTH) // CONV_CH
    hpb = tr // CONV_HALO

    def halo_idx(col):
        return lambda b, i: (jnp.maximum((b * ns + i) * hpb - 1, 0), col)

    vec = lambda a: a.reshape(1, CONV_CH).astype(F32)
    return pl.pallas_call(
        functools.partial(_conv_kernel, tr=tr),
        grid=(batch, ns),
        in_specs=[pl.BlockSpec((tr, CONV_CH), lambda b, i: (b * ns + i, vb)),
                  pl.BlockSpec((tr, CONV_CH), lambda b, i: (b * ns + i, vb + 1)),
                  pl.BlockSpec((CONV_HALO, CONV_CH), halo_idx(vb)),
                  pl.BlockSpec((CONV_HALO, CONV_CH), halo_idx(vb + 1)),
                  pl.BlockSpec((CONV_WIDTH, CONV_CH), lambda b, i: (0, 0)),
                  pl.BlockSpec((1, CONV_CH), lambda b, i: (0, 0)),
                  pl.BlockSpec((1, CONV_CH), lambda b, i: (0, 0)),
                  pl.BlockSpec((1, CONV_CH), lambda b, i: (0, 0))],
        out_specs=pl.BlockSpec((tr, CONV_CH), lambda b, i: (b * ns + i, 0)),
        out_shape=jax.ShapeDtypeStruct((T, CONV_CH), BF16),
        scratch_shapes=[pltpu.VMEM((SUBLANES, CONV_CH // LANES, CONV_HALO + tr, LANES), F32),
                        pltpu.VMEM((CONV_CH // LANES, tr, LANES), F32)],
        compiler_params=_cparams(("parallel", "arbitrary")),
        name=name,
    )(proj, proj, proj, proj, conv_w.astype(F32), vec(conv_b), vec(ln_g), vec(ln_b))


IDX_CHUNK = 1024
SLABS = IDX_CHUNK // LANES
KEY16_LO = -32640
KEY16_HI = 0x7F80
KEY16_BITS = 16


def _key_to_f32(key):
    bits = jnp.where(key < 0, key ^ jnp.int32(0x7FFFFFFF), key)
    return pltpu.bitcast(bits, F32)


def _indexer_kernel(q_ref, k_ref, o_ref, sc_ref, g_ref, *, tq, topk, seq):
    i = pl.program_id(1)
    q0 = i * tq
    nch = (q0 + tq + IDX_CHUNK - 1) // IDX_CHUNK
    n_all = seq // IDX_CHUNK
    lane = lax.broadcasted_iota(jnp.int32, (tq, LANES), 1)

    qh = []
    for blk in range(IDX_Q // LANES):
        qb = q_ref[:, blk * LANES:(blk + 1) * LANES]
        qh.append(jnp.where(lane < IDX_DIM, qb, 0.0).astype(BF16))
        qh.append(jnp.where(lane < IDX_DIM, pltpu.roll(qb, IDX_DIM, 1), 0.0).astype(BF16))
    wcol = IDX_Q + IDX_DIM
    wscale = IDX_HEADS ** -0.5 * IDX_DIM ** -0.5
    wts = [q_ref[:, wcol + h:wcol + h + 1] * wscale for h in range(IDX_HEADS)]

    def score_chunk(c, masked):
        k0 = pl.multiple_of(c * IDX_CHUNK, IDX_CHUNK)
        kc = k_ref[pl.ds(k0, IDX_CHUNK), :]
        sc = None
        for h in range(IDX_HEADS):
            r = lax.dot_general(qh[h], kc, (((1,), (1,)), ((), ())), preferred_element_type=F32)
            t = wts[h] * jnp.maximum(r, 0.0)
            sc = t if sc is None else sc + t
        if masked:
            row = q0 + lax.broadcasted_iota(jnp.int32, (tq, IDX_CHUNK), 0)
            col = k0 + lax.broadcasted_iota(jnp.int32, (tq, IDX_CHUNK), 1)
            sc = jnp.where(col <= row, sc, -jnp.inf)
        hi_bits = pltpu.bitcast(sc, jnp.int32) & jnp.int32(-(1 << KEY16_BITS))
        g = pltpu.bitcast(hi_bits, F32).astype(BF16)
        for j in range(SLABS):
            sc_ref[c * SLABS + j] = sc[:, j * LANES:(j + 1) * LANES]
            g_ref[c * SLABS + j] = g[:, j * LANES:(j + 1) * LANES]

    def load_scores(c):
        return jnp.concatenate([sc_ref[c * SLABS + j] for j in range(SLABS)], axis=1)

    def unmasked_body(c, carry):
        score_chunk(c, False)
        return carry

    lax.fori_loop(0, nch - 1, unmasked_body, 0)
    score_chunk(nch - 1, True)

    def count_in(ref, dtype, pred):
        one = jnp.ones((tq, LANES), dtype)
        zero = jnp.zeros((tq, LANES), dtype)

        def body(c, acc):
            for j in range(SLABS):
                acc = acc + jnp.where(pred(ref[c * SLABS + j]), one, zero)
            return acc
        acc = lax.fori_loop(0, nch, body, zero)
        return jnp.broadcast_to(jnp.sum(acc.astype(F32), axis=1, keepdims=True), (tq, LANES))

    def count(pred):
        return count_in(sc_ref, F32, pred)

    kf = float(topk)
    def bisect16(_, carry):
        lo, hi, cnt_lo = carry
        mid = (lo + hi) >> 1
        key = jnp.where(mid < 0, (mid << KEY16_BITS) | jnp.int32(0xFFFF), mid << KEY16_BITS)
        cand = _key_to_f32(key).astype(BF16)
        cnt = count_in(g_ref, BF16, lambda blk: blk >= cand)
        ok = cnt >= kf
        return jnp.where(ok, mid, lo), jnp.where(ok, hi, mid), jnp.where(ok, cnt, cnt_lo)

    lo16 = jnp.full((tq, LANES), KEY16_LO, jnp.int32)
    hi16 = jnp.full((tq, LANES), KEY16_HI, jnp.int32)
    lo16, _, cnt_lo = lax.fori_loop(0, KEY16_BITS, bisect16, (lo16, hi16, jnp.zeros((tq, LANES), F32)))

    def unresolved(carry):
        lo, hi, _ = carry
        return jnp.max(hi - lo) > 1

    def bisect32(carry):
        lo, hi, cnt_lo = carry
        mid = lo + ((hi - lo) >> 1)
        cand = _key_to_f32(mid)
        cnt = count(lambda blk: blk >= cand)
        ok = cnt >= kf
        hi = jnp.where(cnt == kf, mid + 1, jnp.where(ok, hi, mid))
        return jnp.where(ok, mid, lo), hi, jnp.where(ok, cnt, cnt_lo)

    lo32 = lo16 << KEY16_BITS
    at_start = count(lambda blk: blk > _key_to_f32(lo32)) < kf
    hi32 = jnp.where(at_start, lo32 + 1, (lo16 + 1) << KEY16_BITS)
    lo, _, n_ge = lax.while_loop(unresolved, bisect32, (lo32, hi32, cnt_lo))
    thr = _key_to_f32(lo)
    has_tie = jnp.max(n_ge) > kf
    thr_c = thr[:, :1]

    @pl.when(jnp.logical_not(has_tie))
    def _():
        def body(c, carry):
            k0 = pl.multiple_of(c * IDX_CHUNK, IDX_CHUNK)
            sc = load_scores(c)
            o_ref[0, :, pl.ds(k0, IDX_CHUNK)] = jnp.where(sc >= thr_c, 0.0, NEG).astype(o_ref.dtype)
            return carry
        lax.fori_loop(0, nch, body, 0)

    @pl.when(has_tie)
    def _():
        r_i = lax.broadcasted_iota(jnp.int32, (IDX_CHUNK, IDX_CHUNK), 0)
        c_i = lax.broadcasted_iota(jnp.int32, (IDX_CHUNK, IDX_CHUNK), 1)
        upper = jnp.where(r_i < c_i, 1.0, 0.0).astype(BF16)
        need_c = (kf - count(lambda blk: blk > thr))[:, :1]
        tie_c = jnp.where(n_ge > kf, 1.0, 0.0)[:, :1]

        def pending(carry):
            c, seen = carry
            return (c < nch) & (jnp.max(tie_c * (need_c - seen)) > 0.0)

        def body(carry):
            c, seen = carry
            k0 = pl.multiple_of(c * IDX_CHUNK, IDX_CHUNK)
            sc = load_scores(c)
            eq = jnp.where(sc == thr_c, 1.0, 0.0)
            before = seen + jnp.dot(eq.astype(BF16), upper, preferred_element_type=F32)
            keep = (sc > thr_c) | ((sc == thr_c) & (before < need_c))
            o_ref[0, :, pl.ds(k0, IDX_CHUNK)] = jnp.where(keep, 0.0, NEG).astype(o_ref.dtype)
            return c + 1, seen + jnp.sum(eq, axis=1, keepdims=True)
        c_done, _ = lax.while_loop(pending, body, (jnp.int32(0), jnp.zeros((tq, 1), F32)))

        def rest(c, carry):
            k0 = pl.multiple_of(c * IDX_CHUNK, IDX_CHUNK)
            sc = load_scores(c)
            keep = (sc > thr_c) | ((sc == thr_c) & (tie_c < 0.5))
            o_ref[0, :, pl.ds(k0, IDX_CHUNK)] = jnp.where(keep, 0.0, NEG).astype(o_ref.dtype)
            return carry
        lax.fori_loop(c_done, nch, rest, 0)

    def fill(c, carry):
        k0 = pl.multiple_of(c * IDX_CHUNK, IDX_CHUNK)
        o_ref[0, :, pl.ds(k0, IDX_CHUNK)] = jnp.full((tq, IDX_CHUNK), NEG, o_ref.dtype)
        return carry
    lax.fori_loop(nch, n_all, fill, 0)


def _indexer(idx, kidx, batch, *, tq=128, name):
    T = idx.shape[0]
    S = T // batch
    tq = min(tq, S)
    ns = S // tq
    topk = min(TOPK_MAX, S // 4)
    assert S % IDX_CHUNK == 0 and IDX_CHUNK % tq == 0
    return pl.pallas_call(
        functools.partial(_indexer_kernel, tq=tq, topk=topk, seq=S),
        grid=(batch, ns),
        in_specs=[pl.BlockSpec((tq, IDX_PAD), lambda b, i: (b * ns + i, 0)),
                  pl.BlockSpec((S, LANES), lambda b, i: (b, 0))],
        out_specs=pl.BlockSpec((1, tq, S), lambda b, i: (b, i, 0)),
        out_shape=jax.ShapeDtypeStruct((batch, S, S), BF16),
        scratch_shapes=[pltpu.VMEM((S // LANES, tq, LANES), F32), pltpu.VMEM((S // LANES, tq, LANES), BF16)],
        compiler_params=_cparams(("parallel", "arbitrary")),
        name=name,
    )(idx, kidx)


def _dsa_attn_kernel(qi_ref, kj_ref, q_ref, k_ref, v_ref, bias_ref, o_ref, m_ref, acc_ref, *, tq, tk, rq):
    p_id = pl.program_id(2)
    qi = qi_ref[p_id]
    kj = kj_ref[p_id]
    ratio = tq // tk

    @pl.when(kj == 0)
    def _():
        _flash_init(m_ref, acc_ref)

    def step(d):
        k = k_ref[...]
        v_ext = _with_ones(v_ref[...])
        for rb in range(tq // rq):
            if d is not None and (rb + 1) * rq <= d * tk:
                continue
            rows = slice(rb * rq, (rb + 1) * rq)
            nk = tk if d is None else _causal_width(rb, rq, d, tk)
            bias = bias_ref[0, rows, :nk].astype(F32)
            for r in range(DSA_REP):
                q = q_ref[rows, r * DSA_HEAD_DIM:(r + 1) * DSA_HEAD_DIM]
                _flash_update(q, k[:nk], v_ext[:nk], m_ref, acc_ref, r, rows, bias=bias)

    @pl.when(kj < qi * ratio)
    def _():
        step(None)

    for d in range(ratio):
        @pl.when(kj == qi * ratio + d)
        def _(d=d):
            step(d)

    @pl.when(kj == (qi + 1) * ratio - 1)
    def _():
        for r in range(DSA_REP):
            o = acc_ref[r, :, :DSA_HEAD_DIM] / acc_ref[r, :, DSA_HEAD_DIM:]
            o_ref[:, r * DSA_HEAD_DIM:(r + 1) * DSA_HEAD_DIM] = o.astype(o_ref.dtype)


def _dsa_attn(qkv, bias, batch, *, tq=2048, tk=1024, rq=128, name):
    T = qkv.shape[0]
    S = T // batch
    tq, tk = min(tq, S), min(tk, S)
    rq = min(rq, tq)
    assert S % tq == 0 and tq % tk == 0 and tq % rq == 0
    nq, nk = S // tq, S // tk
    qi, kj = _causal_pairs(nq, tq // tk)
    G = DSA_KV_HEADS
    gw = DSA_REP * DSA_HEAD_DIM
    kb = DSA_Q // DSA_HEAD_DIM
    grid_spec = pltpu.PrefetchScalarGridSpec(
        num_scalar_prefetch=2,
        grid=(batch, G, int(qi.shape[0])),
        in_specs=[pl.BlockSpec((tq, gw), lambda b, g, p, qi, kj: (b * nq + qi[p], g)),
                  pl.BlockSpec((tk, DSA_HEAD_DIM), lambda b, g, p, qi, kj: (b * nk + kj[p], kb + g)),
                  pl.BlockSpec((tk, DSA_HEAD_DIM), lambda b, g, p, qi, kj: (b * nk + kj[p], kb + G + g)),
                  pl.BlockSpec((1, tq, tk), lambda b, g, p, qi, kj: (b, qi[p], kj[p]))],
        out_specs=pl.BlockSpec((tq, gw), lambda b, g, p, qi, kj: (b * nq + qi[p], g)),
        scratch_shapes=[pltpu.VMEM((DSA_REP, tq, DSA_HEAD_DIM), F32),
                        pltpu.VMEM((DSA_REP, tq, 2 * DSA_HEAD_DIM), F32)])
    return pl.pallas_call(
        functools.partial(_dsa_attn_kernel, tq=tq, tk=tk, rq=rq),
        grid_spec=grid_spec,
        out_shape=jax.ShapeDtypeStruct((T, DSA_Q), BF16),
        compiler_params=_cparams(("parallel", "parallel", "arbitrary")),
        name=name,
    )(qi, kj, qkv, qkv, qkv, bias)


def _even_mixer(x, positions, batch, w_in, w_out, lam_p, subln_g, conv_w, conv_b, conv_ln_g, conv_ln_b,
                lam_init, ln_g, ln_b):
    tabs, half = _rot_tables(positions, DIFF_HEAD_DIM)
    n_rot = (2 * DIFF_WIDTH) // LANES
    chunk_types = [0] * n_rot + [-1] * ((EVEN_IN - 2 * DIFF_WIDTH) // LANES)
    q_scale = DIFF_HEAD_DIM ** -0.5 * LOG2E
    chunk_scales = [q_scale] * (DIFF_WIDTH // LANES) + [1.0] * ((EVEN_IN - DIFF_WIDTH) // LANES)
    proj = _inproj(x, w_in.astype(BF16), tabs, (half,), chunk_types, BF16, tm=1024, tn=1024, name="even_inproj",
                   chunk_scales=chunk_scales)
    lp = lam_p.astype(F32)
    lam = jnp.exp(jnp.sum(lp[0] * lp[1])) - jnp.exp(jnp.sum(lp[2] * lp[3])) + lam_init
    a = _diff_attn(proj, lam, subln_g, batch, lam_init, name="diff_attn")
    c = _conv_module(proj, conv_w, conv_b, conv_ln_g, conv_ln_b, batch, name="conv_module")
    return _proj_ln((a, c), w_out.astype(BF16), x, ln_g, ln_b, name="even_outproj_ln")


def _odd_mixer(x, positions, batch, w_in, w_out, ln_g, ln_b):
    T = x.shape[0]
    tabs_b, half_b = _rot_tables(positions, DSA_HEAD_DIM)
    n_rot = (DSA_Q + DSA_KV) // LANES
    chunk_types = [0] * n_rot + [-1] * (DSA_KV // LANES)
    w_main = w_in[:, :ODD_MAIN].astype(BF16)
    q_scale = DSA_HEAD_DIM ** -0.5 * LOG2E
    chunk_scales = [q_scale] * (DSA_Q // LANES) + [1.0] * ((ODD_MAIN - DSA_Q) // LANES)
    qkv = _inproj(x, w_main, tabs_b, (half_b,), chunk_types, BF16, tm=1024, tn=1024, name="odd_inproj",
                  chunk_scales=chunk_scales)
    w_idx = jnp.pad(w_in[:, ODD_MAIN:], ((0, 0), (0, IDX_PAD - (w_in.shape[1] - ODD_MAIN)))).astype(BF16)
    tabs_a, half_a = _rot_tables(positions, IDX_DIM)
    tabs_ah, _ = _rot_tables(positions, IDX_DIM, active_lanes=IDX_DIM)
    idx_types = [0] * (IDX_Q // LANES) + [1] + [-1] * ((IDX_PAD - IDX_Q) // LANES - 1)
    idx = _inproj(x, w_idx, jnp.concatenate([tabs_a, tabs_ah]), (half_a, half_a), idx_types, F32,
                  tm=1024, tn=IDX_PAD, name="idx_inproj")
    kidx = idx[:, IDX_Q:IDX_Q + LANES].astype(BF16)
    bias = _indexer(idx, kidx, batch, name="indexer")
    o = _dsa_attn(qkv, bias, batch, name="dsa_attn")
    return _proj_ln((o,), w_out.astype(BF16), x, ln_g, ln_b, name="odd_outproj_ln")


def kernel(x, mem, positions, w_in_even, w_out_even, diff_lambda, diff_subln_g, conv_w, conv_b, conv_ln_g,
           conv_ln_b, w_in_odd, w_out_odd, xa_wq, xa_wkv, xa_wo, ffn_w_in, ffn_w_out, ln_g, ln_b):
    B, S, D = x.shape
    M = mem.shape[1]
    x = x.reshape(B * S, D)
    mem2 = mem.reshape(B * M, D)
    for layer in range(DEPTH):
        j = layer // 2
        if layer % 2 == 0:
            lam_init = 0.8 - 0.6 * math.exp(-0.3 * layer)
            x = _even_mixer(x, positions, B, w_in_even[j], w_out_even[j], diff_lambda[j], diff_subln_g[j],
                            conv_w[j], conv_b[j], conv_ln_g[j], conv_ln_b[j], lam_init,
                            ln_g[layer, 0], ln_b[layer, 0])
        else:
            x = _odd_mixer(x, positions, B, w_in_odd[j], w_out_odd[j], ln_g[layer, 0], ln_b[layer, 0])
        kv = _inproj(mem2, xa_wkv[layer].astype(BF16), None, (), [-1] * (2 * D // LANES), BF16,
                     tm=512, tn=512, name=f"xa_kvproj_{layer}")
        o = _xattn(x, xa_wq[layer].astype(BF16), kv, B, name=f"xattn_{layer}")
        x = _proj_ln((o,), xa_wo[layer].astype(BF16), x, ln_g[layer, 1], ln_b[layer, 1],
                     name=f"xa_outproj_ln_{layer}")
        x = _ffn(x, ffn_w_in[layer].astype(BF16), ffn_w_out[layer].astype(BF16),
                 ln_g[layer, 2], ln_b[layer, 2], name=f"ffn_{layer}")
    return x.reshape(B, S, D)
```

```python
import functools
import math

import jax
import jax.numpy as jnp
from jax import lax
from jax.experimental import pallas as pl
from jax.experimental.pallas import tpu as pltpu

F32 = jnp.float32
BF16 = jnp.bfloat16

D_MODEL = 2048
DEPTH = 2
ALPHA = (2 * DEPTH) ** 0.25
LN_EPS = 1e-5
ROPE_THETA = 500000.0
ROPE_FRAC = 4

DIFF_HEADS = 8
DIFF_HEAD_DIM = 64
DIFF_V_DIM = 2 * DIFF_HEAD_DIM
DIFF_WIDTH = DIFF_HEADS * DIFF_V_DIM
CONV_CH = D_MODEL - DIFF_WIDTH
CONV_WIDTH = 31
EVEN_IN = 3 * DIFF_WIDTH + 2 * CONV_CH

DSA_HEADS = 16
DSA_KV_HEADS = 4
DSA_HEAD_DIM = 128
DSA_REP = DSA_HEADS // DSA_KV_HEADS
IDX_HEADS = 4
IDX_DIM = 64
TOPK_MAX = 256
DSA_Q = DSA_HEADS * DSA_HEAD_DIM
DSA_KV = DSA_KV_HEADS * DSA_HEAD_DIM
IDX_Q = IDX_HEADS * IDX_DIM
ODD_MAIN = DSA_Q + 2 * DSA_KV
IDX_PAD = 512

XA_HEADS = 4
XA_HEAD_DIM = D_MODEL // XA_HEADS
D_FF = -(-(8 * D_MODEL) // (3 * 256)) * 256

LANES = 128
SUBLANES = 8
MXU_COLS = 256
NEG = -1e30
VMEM_LIMIT = 56 * 1024 * 1024


def _cparams(sem):
    return pltpu.CompilerParams(dimension_semantics=sem, vmem_limit_bytes=VMEM_LIMIT)


def _layer_norm_rows(y, g, b):
    mu = jnp.mean(y, axis=-1, keepdims=True)
    d = y - mu
    var = jnp.mean(d * d, axis=-1, keepdims=True)
    return d * lax.rsqrt(var + LN_EPS) * g + b


def _rot_tables(positions, head_dim, active_lanes=LANES):
    rot = head_dim // ROPE_FRAC
    half = rot // 2
    inv_freq = 1.0 / (ROPE_THETA ** (jnp.arange(half, dtype=F32) / half))
    ang = positions.reshape(-1, 1).astype(F32) * inv_freq
    cos, sin = jnp.cos(ang), jnp.sin(ang)
    lane = jnp.arange(LANES)
    d = lane % head_dim
    f = d % half
    live = lane < active_lanes
    cos_l = jnp.take(cos, f, axis=1)
    sin_l = jnp.take(sin, f, axis=1)
    c = jnp.where((d < rot) & live, cos_l, 1.0)
    s1 = jnp.where((d >= half) & (d < rot) & live, sin_l, 0.0)
    s2 = jnp.where((d < half) & live, -sin_l, 0.0)
    return jnp.stack([c, s1, s2]), half


def _inproj_kernel(x_ref, w_ref, tab_ref, o_ref, xb_ref, *, variants, shifts):
    j = pl.program_id(1)

    @pl.when(j == 0)
    def _():
        xb_ref[...] = x_ref[...].astype(BF16)

    acc = jnp.dot(xb_ref[...], w_ref[...], preferred_element_type=F32)

    for (j_lo, j_hi), types in variants:
        @pl.when((j >= j_lo) & (j < j_hi))
        def _(types=types):
            for c, (p, scale) in enumerate(types):
                blk = acc[:, c * LANES:(c + 1) * LANES]
                if p >= 0:
                    sh = shifts[p]
                    blk = (blk * tab_ref[3 * p]
                           + pltpu.roll(blk, sh, 1) * tab_ref[3 * p + 1]
                           + pltpu.roll(blk, LANES - sh, 1) * tab_ref[3 * p + 2])
                if scale != 1.0:
                    blk = blk * scale
                o_ref[:, c * LANES:(c + 1) * LANES] = blk.astype(o_ref.dtype)


def _inproj(x, w, tabs, shifts, chunk_types, out_dtype, *, tm, tn, name, chunk_scales=None):
    T, K = x.shape
    N = w.shape[1]
    tm = min(tm, T)
    assert T % tm == 0 and N % tn == 0 and len(chunk_types) == N // LANES
    if chunk_scales is None:
        chunk_scales = [1.0] * len(chunk_types)
    chunk_types = list(zip(chunk_types, chunk_scales))
    per = tn // LANES
    tile_types = [tuple(chunk_types[t * per:(t + 1) * per]) for t in range(N // tn)]
    variants = []
    for t, ty in enumerate(tile_types):
        if variants and variants[-1][1] == ty and variants[-1][0][1] == t:
            variants[-1] = ((variants[-1][0][0], t + 1), ty)
        else:
            variants.append(((t, t + 1), ty))
    if tabs is None:
        tabs = jnp.zeros((3, T, LANES), F32)
    P3 = tabs.shape[0]
    return pl.pallas_call(
        functools.partial(_inproj_kernel, variants=tuple(variants), shifts=tuple(shifts)),
        grid=(T // tm, N // tn),
        in_specs=[pl.BlockSpec((tm, K), lambda i, j: (i, 0)),
                  pl.BlockSpec((K, tn), lambda i, j: (0, j)),
                  pl.BlockSpec((P3, tm, LANES), lambda i, j: (0, i, 0))],
        out_specs=pl.BlockSpec((tm, tn), lambda i, j: (i, j)),
        out_shape=jax.ShapeDtypeStruct((T, N), out_dtype),
        scratch_shapes=[pltpu.VMEM((tm, K), BF16)],
        compiler_params=_cparams(("parallel", "arbitrary")),
        name=name,
    )(x, w, tabs)


def _proj_ln_kernel(*refs, widths):
    h_refs = refs[:len(widths)]
    w_ref, x_ref, g_ref, b_ref, o_ref = refs[len(widths):]
    y = ALPHA * x_ref[...]
    k0 = 0
    for h_ref, kw in zip(h_refs, widths):
        y = y + jnp.dot(h_ref[...], w_ref[k0:k0 + kw, :], preferred_element_type=F32)
        k0 += kw
    o_ref[...] = _layer_norm_rows(y, g_ref[...], b_ref[...])


def _proj_ln(hs, w, x, g, b, *, tm=512, name):
    T = x.shape[0]
    K, D = w.shape
    widths = tuple(h.shape[1] for h in hs)
    tm = min(tm, T)
    assert T % tm == 0 and sum(widths) == K
    return pl.pallas_call(
        functools.partial(_proj_ln_kernel, widths=widths),
        grid=(T // tm,),
        in_specs=[pl.BlockSpec((tm, kw), lambda i: (i, 0)) for kw in widths] + [
                  pl.BlockSpec((K, D), lambda i: (0, 0), pipeline_mode=pl.Buffered(1)),
                  pl.BlockSpec((tm, D), lambda i: (i, 0)),
                  pl.BlockSpec((1, D), lambda i: (0, 0)),
                  pl.BlockSpec((1, D), lambda i: (0, 0))],
        out_specs=pl.BlockSpec((tm, D), lambda i: (i, 0)),
        out_shape=jax.ShapeDtypeStruct((T, D), F32),
        compiler_params=_cparams(("parallel",)),
        name=name,
    )(*hs, w, x, g.reshape(1, D), b.reshape(1, D))


def _ffn_kernel(x_ref, wg_ref, wu_ref, wo_ref, g_ref, b_ref, o_ref, xb_ref, acc_ref):
    f = pl.program_id(1)

    @pl.when(f == 0)
    def _():
        xb_ref[...] = x_ref[...].astype(BF16)
        acc_ref[...] = jnp.zeros_like(acc_ref)

    xb = xb_ref[...]
    gate = jnp.dot(xb, wg_ref[...], preferred_element_type=F32)
    up = jnp.dot(xb, wu_ref[...], preferred_element_type=F32)
    h = (gate * jax.nn.sigmoid(gate) * up).astype(BF16)
    acc_ref[...] += jnp.dot(h, wo_ref[...], preferred_element_type=F32)

    @pl.when(f == pl.num_programs(1) - 1)
    def _():
        y = ALPHA * x_ref[...] + acc_ref[...]
        o_ref[...] = _layer_norm_rows(y, g_ref[...], b_ref[...])


def _ffn(x, w_in, w_out, g, b, *, tm=512, tf=512, name):
    T, D = x.shape
    dff = w_out.shape[0]
    tm = min(tm, T)
    assert T % tm == 0 and dff % tf == 0
    nf = dff // tf
    return pl.pallas_call(
        _ffn_kernel,
        grid=(T // tm, nf),
        in_specs=[pl.BlockSpec((tm, D), lambda i, f: (i, 0)),
                  pl.BlockSpec((D, tf), lambda i, f: (0, f)),
                  pl.BlockSpec((D, tf), lambda i, f: (0, f + nf)),
                  pl.BlockSpec((tf, D), lambda i, f: (f, 0)),
                  pl.BlockSpec((1, D), lambda i, f: (0, 0)),
                  pl.BlockSpec((1, D), lambda i, f: (0, 0))],
        out_specs=pl.BlockSpec((tm, D), lambda i, f: (i, 0)),
        out_shape=jax.ShapeDtypeStruct((T, D), F32),
        scratch_shapes=[pltpu.VMEM((tm, D), BF16), pltpu.VMEM((tm, D), F32)],
        compiler_params=_cparams(("parallel", "arbitrary")),
        name=name,
    )(x, w_in, w_in, w_out, g.reshape(1, D), b.reshape(1, D))


def _xattn_kernel(x_ref, wq_ref, k_ref, v_ref, o_ref):
    xb = x_ref[...].astype(BF16)
    q = jnp.dot(xb, wq_ref[...], preferred_element_type=F32).astype(BF16)
    scale = XA_HEAD_DIM ** -0.5
    for h in range(XA_HEADS):
        sl = slice(h * XA_HEAD_DIM, (h + 1) * XA_HEAD_DIM)
        s = lax.dot_general(q[:, sl], k_ref[:, sl], (((1,), (1,)), ((), ())),
                            preferred_element_type=F32) * scale
        m = jnp.max(s, axis=-1, keepdims=True)
        p = jnp.exp(s - m)
        l = jnp.sum(p, axis=-1, keepdims=True)
        p = (p / l).astype(BF16)
        o = jnp.dot(p, v_ref[:, sl], preferred_element_type=F32)
        o_ref[:, sl] = o.astype(o_ref.dtype)


def _xattn(x, wq, kv, batch, *, tm=512, name):
    T, D = x.shape
    S = T // batch
    M = kv.shape[0] // batch
    tm = min(tm, S)
    ns = S // tm
    return pl.pallas_call(
        _xattn_kernel,
        grid=(batch, ns),
        in_specs=[pl.BlockSpec((tm, D), lambda b, i: (b * ns + i, 0)),
                  pl.BlockSpec((D, D), lambda b, i: (0, 0)),
                  pl.BlockSpec((M, D), lambda b, i: (b, 0)),
                  pl.BlockSpec((M, D), lambda b, i: (b, 1))],
        out_specs=pl.BlockSpec((tm, D), lambda b, i: (b * ns + i, 0)),
        out_shape=jax.ShapeDtypeStruct((T, D), BF16),
        compiler_params=_cparams(("parallel", "arbitrary")),
        name=name,
    )(x, wq, kv, kv)


NT_DIMS = (((1,), (1,)), ((), ()))
LOG2E = math.log2(math.e)


def _causal_pairs(nq, ratio):
    qi = [i for i in range(nq) for _ in range((i + 1) * ratio)]
    kj = [j for i in range(nq) for j in range((i + 1) * ratio)]
    return jnp.asarray(qi, jnp.int32), jnp.asarray(kj, jnp.int32)


def _flash_update(q, k, v_ext, m_ref, acc_ref, idx, rows, bias=None, mask=None):
    s = lax.dot_general(q, k, NT_DIMS, preferred_element_type=F32)
    if bias is not None:
        s = s + bias
    if mask is not None:
        s = jnp.where(mask, s, NEG)
    chunks = [s[:, j * LANES:(j + 1) * LANES] for j in range(s.shape[1] // LANES)]
    mc = chunks[0]
    for ch in chunks[1:]:
        mc = jnp.maximum(mc, ch)
    m_prev = m_ref[idx, rows, :]
    m_new = jnp.maximum(m_prev, jnp.max(mc, axis=1, keepdims=True))
    p = jnp.concatenate([jnp.exp2(ch - m_new).astype(BF16) for ch in chunks], axis=1)
    alpha = jnp.exp2(m_prev - m_new)
    pv = jnp.dot(p, v_ext, preferred_element_type=F32)
    acc_ref[idx, rows, :] = jnp.concatenate([alpha, alpha], axis=1) * acc_ref[idx, rows, :] + pv
    m_ref[idx, rows, :] = m_new


def _causal_width(rb, rq, d, tk):
    seen = (rb + 1) * rq - d * tk
    return min(tk, -(-seen // MXU_COLS) * MXU_COLS)


def _flash_init(m_ref, acc_ref):
    m_ref[...] = jnp.full_like(m_ref, -jnp.inf)
    acc_ref[...] = jnp.zeros_like(acc_ref)


def _with_ones(v):
    return jnp.concatenate([v, jnp.ones_like(v)], axis=1)


def _diff_attn_kernel(lam_ref, q_ref, k_ref, v_ref, g_ref, o_ref, m_ref, acc_ref, *, tq, tk, rq, lam_init):
    qi = pl.program_id(2)
    ratio = tq // tk
    _flash_init(m_ref, acc_ref)

    def step(kj, d):
        k0 = pl.multiple_of(kj * tk, tk)
        k = k_ref[pl.ds(k0, tk), :]
        v_ext = _with_ones(v_ref[pl.ds(k0, tk), :])
        lane = lax.broadcasted_iota(jnp.int32, (rq, LANES), 1)
        for rb in range(tq // rq):
            if d is not None and (rb + 1) * rq <= d * tk:
                continue
            rows = slice(rb * rq, (rb + 1) * rq)
            q = q_ref[rows, :]
            mask = None
            nk = tk if d is None else _causal_width(rb, rq, d, tk)
            if d is not None and rb * rq < d * tk + nk - 1:
                row = rb * rq + lax.broadcasted_iota(jnp.int32, (rq, nk), 0)
                col = d * tk + lax.broadcasted_iota(jnp.int32, (rq, nk), 1)
                mask = col <= row
            for c, sel in enumerate((lane < DIFF_HEAD_DIM, lane >= DIFF_HEAD_DIM)):
                _flash_update(jnp.where(sel, q, jnp.zeros_like(q)), k[:nk], v_ext[:nk], m_ref, acc_ref, c, rows,
                              mask=mask)

    def below_diagonal(kj, carry):
        step(kj, None)
        return carry

    lax.fori_loop(0, qi * ratio, below_diagonal, 0)
    for d in range(ratio):
        step(qi * ratio + d, d)

    lam = lam_ref[0]
    a = (acc_ref[0, :, :LANES] / acc_ref[0, :, LANES:]
         - lam * (acc_ref[1, :, :LANES] / acc_ref[1, :, LANES:]))
    ms = jnp.mean(a * a, axis=-1, keepdims=True)
    a = a * lax.rsqrt(ms + LN_EPS) * g_ref[...]
    o_ref[...] = (a * (1.0 - lam_init)).astype(o_ref.dtype)


def _diff_attn(proj, lam, subln_g, batch, lam_init, *, tq=2048, tk=1024, rq=128, name):
    T = proj.shape[0]
    S = T // batch
    tq, tk = min(tq, S), min(tk, S)
    rq = min(rq, tq)
    assert S % tq == 0 and tq % tk == 0 and tq % rq == 0
    nq = S // tq
    H = DIFF_HEADS
    return pl.pallas_call(
        functools.partial(_diff_attn_kernel, tq=tq, tk=tk, rq=rq, lam_init=lam_init),
        grid=(batch, H, nq),
        in_specs=[pl.BlockSpec(memory_space=pltpu.SMEM),
                  pl.BlockSpec((tq, LANES), lambda b, h, i: (b * nq + i, h)),
                  pl.BlockSpec((S, LANES), lambda b, h, i: (b, H + h)),
                  pl.BlockSpec((S, LANES), lambda b, h, i: (b, 2 * H + h)),
                  pl.BlockSpec((1, LANES), lambda b, h, i: (0, 0))],
        out_specs=pl.BlockSpec((tq, LANES), lambda b, h, i: (b * nq + i, h)),
        out_shape=jax.ShapeDtypeStruct((T, DIFF_WIDTH), BF16),
        scratch_shapes=[pltpu.VMEM((2, tq, LANES), F32), pltpu.VMEM((2, tq, 2 * LANES), F32)],
        compiler_params=_cparams(("parallel", "parallel", "arbitrary")),
        name=name,
    )(lam.reshape(1).astype(F32), proj, proj, proj, subln_g.reshape(1, LANES).astype(F32))


CONV_HALO = 32
CONV_ROWS = 64


def _conv_kernel(val_ref, gate_ref, hval_ref, hgate_ref, cw_ref, cb_ref, g_ref, b_ref, o_ref,
                 u_ref, c_ref, *, tr):
    i = pl.program_id(1)
    rows = CONV_HALO + tr
    n_ch = CONV_CH // LANES
    halo_on = jnp.where(i > 0, 1.0, 0.0)
    for c in range(n_ch):
        cs = slice(c * LANES, (c + 1) * LANES)
        u_ref[0, c, CONV_HALO:rows, :] = val_ref[:, cs].astype(F32) * jax.nn.sigmoid(gate_ref[:, cs].astype(F32))
        halo = hval_ref[:, cs].astype(F32) * jax.nn.sigmoid(hgate_ref[:, cs].astype(F32))
        u_ref[0, c, :CONV_HALO, :] = halo * halo_on
        for e in range(1, SUBLANES):
            u_ref[e, c, :rows - SUBLANES, :] = u_ref[0, c, e:rows - SUBLANES + e, :]
    base = CONV_HALO - (CONV_WIDTH - 1)
    n_rb = tr // CONV_ROWS
    for c in range(n_ch):
        cs = slice(c * LANES, (c + 1) * LANES)
        accs = [jnp.broadcast_to(cb_ref[:, cs], (CONV_ROWS, LANES))] * n_rb
        for w in range(CONV_WIDTH):
            e = (base + w) % SUBLANES
            tap = jnp.broadcast_to(cw_ref[w:w + 1, cs], (CONV_ROWS, LANES))
            for r in range(n_rb):
                r0 = r * CONV_ROWS + base + w - e
                accs[r] = accs[r] + u_ref[e, c, r0:r0 + CONV_ROWS, :] * tap
        for r in range(n_rb):
            c_ref[c, r * CONV_ROWS:(r + 1) * CONV_ROWS, :] = accs[r]
    conv = jnp.concatenate([c_ref[c] for c in range(n_ch)], axis=1)
    y = _layer_norm_rows(conv, g_ref[...], b_ref[...])
    o_ref[...] = (y * jax.nn.sigmoid(y)).astype(o_ref.dtype)


def _conv_module(proj, conv_w, conv_b, ln_g, ln_b, batch, *, tr=256, name):
    T = proj.shape[0]
    S = T // batch
    tr = min(tr, S)
    ns = S // tr
    vb = (3 * DIFF_WIDTH) // CONV_CH
    hpb = tr // CONV_HALO

    def halo_idx(col):
        return lambda b, i: (jnp.maximum((b * ns + i) * hpb - 1, 0), col)

    vec = lambda a: a.reshape(1, CONV_CH).astype(F32)
    return pl.pallas_call(
        functools.partial(_conv_kernel, tr=tr),
        grid=(batch, ns),
        in_specs=[pl.BlockSpec((tr, CONV_CH), lambda b, i: (b * ns + i, vb)),
                  pl.BlockSpec((tr, CONV_CH), lambda b, i: (b * ns + i, vb + 1)),
                  pl.BlockSpec((CONV_HALO, CONV_CH), halo_idx(vb)),
                  pl.BlockSpec((CONV_HALO, CONV_CH), halo_idx(vb + 1)),
                  pl.BlockSpec((CONV_WIDTH, CONV_CH), lambda b, i: (0, 0)),
                  pl.BlockSpec((1, CONV_CH), lambda b, i: (0, 0)),
                  pl.BlockSpec((1, CONV_CH), lambda b, i: (0, 0)),
                  pl.BlockSpec((1, CONV_CH), lambda b, i: (0, 0))],
        out_specs=pl.BlockSpec((tr, CONV_CH), lambda b, i: (b * ns + i, 0)),
        out_shape=jax.ShapeDtypeStruct((T, CONV_CH), BF16),
        scratch_shapes=[pltpu.VMEM((SUBLANES, CONV_CH // LANES, CONV_HALO + tr, LANES), F32),
                        pltpu.VMEM((CONV_CH // LANES, tr, LANES), F32)],
        compiler_params=_cparams(("parallel", "arbitrary")),
        name=name,
    )(proj, proj, proj, proj, conv_w.astype(F32), vec(conv_b), vec(ln_g), vec(ln_b))


IDX_CHUNK = 1024
SLABS = IDX_CHUNK // LANES
KEY16_LO = -32640
KEY16_HI = 0x7F80
KEY16_BITS = 16
GALLOP_START = 256


def _key_to_f32(key):
    bits = jnp.where(key < 0, key ^ jnp.int32(0x7FFFFFFF), key)
    return pltpu.bitcast(bits, F32)


def _indexer_kernel(q_ref, k_ref, o_ref, sc_ref, g_ref, *, tq, topk, seq):
    i = pl.program_id(1)
    q0 = i * tq
    nch = (q0 + tq + IDX_CHUNK - 1) // IDX_CHUNK
    n_all = seq // IDX_CHUNK
    lane = lax.broadcasted_iota(jnp.int32, (tq, LANES), 1)

    qh = []
    for blk in range(IDX_Q // LANES):
        qb = q_ref[:, blk * LANES:(blk + 1) * LANES]
        qh.append(jnp.where(lane < IDX_DIM, qb, 0.0).astype(BF16))
        qh.append(jnp.where(lane < IDX_DIM, pltpu.roll(qb, IDX_DIM, 1), 0.0).astype(BF16))
    wcol = IDX_Q + IDX_DIM
    wscale = IDX_HEADS ** -0.5 * IDX_DIM ** -0.5
    wts = [q_ref[:, wcol + h:wcol + h + 1] * wscale for h in range(IDX_HEADS)]

    def score_chunk(c, masked):
        k0 = pl.multiple_of(c * IDX_CHUNK, IDX_CHUNK)
        kc = k_ref[pl.ds(k0, IDX_CHUNK), :]
        sc = None
        for h in range(IDX_HEADS):
            r = lax.dot_general(qh[h], kc, (((1,), (1,)), ((), ())), preferred_element_type=F32)
            t = wts[h] * jnp.maximum(r, 0.0)
            sc = t if sc is None else sc + t
        if masked:
            row = q0 + lax.broadcasted_iota(jnp.int32, (tq, IDX_CHUNK), 0)
            col = k0 + lax.broadcasted_iota(jnp.int32, (tq, IDX_CHUNK), 1)
            sc = jnp.where(col <= row, sc, -jnp.inf)
        hi_bits = pltpu.bitcast(sc, jnp.int32) & jnp.int32(-(1 << KEY16_BITS))
        g = pltpu.bitcast(hi_bits, F32).astype(BF16)
        for j in range(SLABS):
            sc_ref[c * SLABS + j] = sc[:, j * LANES:(j + 1) * LANES]
            g_ref[c * SLABS + j] = g[:, j * LANES:(j + 1) * LANES]

    def load_scores(c):
        return jnp.concatenate([sc_ref[c * SLABS + j] for j in range(SLABS)], axis=1)

    def unmasked_body(c, carry):
        score_chunk(c, False)
        return carry

    lax.fori_loop(0, nch - 1, unmasked_body, 0)
    score_chunk(nch - 1, True)

    def count_in(ref, dtype, pred):
        one = jnp.ones((tq, LANES), dtype)
        zero = jnp.zeros((tq, LANES), dtype)

        def body(c, acc):
            for j in range(SLABS):
                acc = acc + jnp.where(pred(ref[c * SLABS + j]), one, zero)
            return acc
        acc = lax.fori_loop(0, nch, body, zero)
        return jnp.broadcast_to(jnp.sum(acc.astype(F32), axis=1, keepdims=True), (tq, LANES))

    def count(pred):
        return count_in(sc_ref, F32, pred)

    kf = float(topk)
    def max_body(c, acc):
        for j in range(SLABS):
            acc = jnp.maximum(acc, g_ref[c * SLABS + j])
        return acc
    gmax = lax.fori_loop(0, nch, max_body, jnp.full((tq, LANES), -jnp.inf, BF16))
    gmax = jnp.broadcast_to(jnp.max(gmax.astype(F32), axis=1, keepdims=True), (tq, LANES))
    mbits = pltpu.bitcast(gmax, jnp.int32)
    mkey = jnp.where(mbits < 0, mbits ^ jnp.int32(0x7FFFFFFF), mbits)
    hi16 = jnp.minimum((mkey >> KEY16_BITS) + 1, KEY16_HI)

    def searching16(carry):
        lo, hi, _, gallop, _ = carry
        return jnp.max(jnp.where(gallop > 0, 2, hi - lo)) > 1

    def search16(carry):
        lo, hi, step, gallop, cnt_lo = carry
        probe = jnp.where(gallop > 0, jnp.maximum(hi - step, KEY16_LO), (lo + hi) >> 1)
        key = jnp.where(probe < 0, (probe << KEY16_BITS) | jnp.int32(0xFFFF), probe << KEY16_BITS)
        cand = _key_to_f32(key).astype(BF16)
        cnt = count_in(g_ref, BF16, lambda blk: blk >= cand)
        ok = cnt >= kf
        at_floor = jnp.where(ok, 0, jnp.where(probe == KEY16_LO, gallop, 0))
        lo = jnp.where(ok, probe, lo)
        hi = jnp.where(ok, hi, jnp.where(at_floor > 0, KEY16_LO + 1, probe))
        step = jnp.where(ok, step, jnp.minimum(step * 2, 1 << KEY16_BITS))
        gallop = jnp.where(ok, 0, jnp.where(at_floor > 0, 0, gallop))
        return lo, hi, step, gallop, jnp.where(ok, cnt, cnt_lo)

    lo16, _, _, _, cnt_lo = lax.while_loop(
        searching16, search16,
        (jnp.full((tq, LANES), KEY16_LO, jnp.int32), hi16, jnp.full((tq, LANES), GALLOP_START, jnp.int32),
         jnp.ones((tq, LANES), jnp.int32), jnp.zeros((tq, LANES), F32)))

    def unresolved(carry):
        lo, hi, _ = carry
        return jnp.max(hi - lo) > 1

    def bisect32(carry):
        lo, hi, cnt_lo = carry
        mid = lo + ((hi - lo) >> 1)
        cand = _key_to_f32(mid)
        cnt = count(lambda blk: blk >= cand)
        ok = cnt >= kf
        hi = jnp.where(cnt == kf, mid + 1, jnp.where(ok, hi, mid))
        return jnp.where(ok, mid, lo), hi, jnp.where(ok, cnt, cnt_lo)

    lo32 = lo16 << KEY16_BITS
    at_start = count(lambda blk: blk > _key_to_f32(lo32)) < kf
    hi32 = jnp.where(at_start, lo32 + 1, (lo16 + 1) << KEY16_BITS)
    lo, _, n_ge = lax.while_loop(unresolved, bisect32, (lo32, hi32, cnt_lo))
    thr = _key_to_f32(lo)
    has_tie = jnp.max(n_ge) > kf
    thr_c = thr[:, :1]

    @pl.when(jnp.logical_not(has_tie))
    def _():
        def body(c, carry):
            k0 = pl.multiple_of(c * IDX_CHUNK, IDX_CHUNK)
            sc = load_scores(c)
            o_ref[0, :, pl.ds(k0, IDX_CHUNK)] = jnp.where(sc >= thr_c, 0.0, NEG).astype(o_ref.dtype)
            return carry
        lax.fori_loop(0, nch, body, 0)

    @pl.when(has_tie)
    def _():
        r_i = lax.broadcasted_iota(jnp.int32, (IDX_CHUNK, IDX_CHUNK), 0)
        c_i = lax.broadcasted_iota(jnp.int32, (IDX_CHUNK, IDX_CHUNK), 1)
        upper = jnp.where(r_i < c_i, 1.0, 0.0).astype(BF16)
        need_c = (kf - count(lambda blk: blk > thr))[:, :1]
        tie_c = jnp.where(n_ge > kf, 1.0, 0.0)[:, :1]

        def pending(carry):
            c, seen = carry
            return (c < nch) & (jnp.max(tie_c * (need_c - seen)) > 0.0)

        def body(carry):
            c, seen = carry
            k0 = pl.multiple_of(c * IDX_CHUNK, IDX_CHUNK)
            sc = load_scores(c)
            eq = jnp.where(sc == thr_c, 1.0, 0.0)
            before = seen + jnp.dot(eq.astype(BF16), upper, preferred_element_type=F32)
            keep = (sc > thr_c) | ((sc == thr_c) & (before < need_c))
            o_ref[0, :, pl.ds(k0, IDX_CHUNK)] = jnp.where(keep, 0.0, NEG).astype(o_ref.dtype)
            return c + 1, seen + jnp.sum(eq, axis=1, keepdims=True)
        c_done, _ = lax.while_loop(pending, body, (jnp.int32(0), jnp.zeros((tq, 1), F32)))

        def rest(c, carry):
            k0 = pl.multiple_of(c * IDX_CHUNK, IDX_CHUNK)
            sc = load_scores(c)
            keep = (sc > thr_c) | ((sc == thr_c) & (tie_c < 0.5))
            o_ref[0, :, pl.ds(k0, IDX_CHUNK)] = jnp.where(keep, 0.0, NEG).astype(o_ref.dtype)
            return carry
        lax.fori_loop(c_done, nch, rest, 0)

    def fill(c, carry):
        k0 = pl.multiple_of(c * IDX_CHUNK, IDX_CHUNK)
        o_ref[0, :, pl.ds(k0, IDX_CHUNK)] = jnp.full((tq, IDX_CHUNK), NEG, o_ref.dtype)
        return carry
    lax.fori_loop(nch, n_all, fill, 0)


def _indexer(idx, kidx, batch, *, tq=128, name):
    T = idx.shape[0]
    S = T // batch
    tq = min(tq, S)
    ns = S // tq
    topk = min(TOPK_MAX, S // 4)
    assert S % IDX_CHUNK == 0 and IDX_CHUNK % tq == 0
    return pl.pallas_call(
        functools.partial(_indexer_kernel, tq=tq, topk=topk, seq=S),
        grid=(batch, ns),
        in_specs=[pl.BlockSpec((tq, IDX_PAD), lambda b, i: (b * ns + i, 0)),
                  pl.BlockSpec((S, LANES), lambda b, i: (b, 0))],
        out_specs=pl.BlockSpec((1, tq, S), lambda b, i: (b, i, 0)),
        out_shape=jax.ShapeDtypeStruct((batch, S, S), BF16),
        scratch_shapes=[pltpu.VMEM((S // LANES, tq, LANES), F32), pltpu.VMEM((S // LANES, tq, LANES), BF16)],
        compiler_params=_cparams(("parallel", "arbitrary")),
        name=name,
    )(idx, kidx)


def _dsa_attn_kernel(qi_ref, kj_ref, q_ref, k_ref, v_ref, bias_ref, o_ref, m_ref, acc_ref, *, tq, tk, rq):
    p_id = pl.program_id(2)
    qi = qi_ref[p_id]
    kj = kj_ref[p_id]
    ratio = tq // tk

    @pl.when(kj == 0)
    def _():
        _flash_init(m_ref, acc_ref)

    def step(d):
        k = k_ref[...]
        v_ext = _with_ones(v_ref[...])
        for rb in range(tq // rq):
            if d is not None and (rb + 1) * rq <= d * tk:
                continue
            rows = slice(rb * rq, (rb + 1) * rq)
            nk = tk if d is None else _causal_width(rb, rq, d, tk)
            bias = bias_ref[0, rows, :nk].astype(F32)
            for r in range(DSA_REP):
                q = q_ref[rows, r * DSA_HEAD_DIM:(r + 1) * DSA_HEAD_DIM]
                _flash_update(q, k[:nk], v_ext[:nk], m_ref, acc_ref, r, rows, bias=bias)

    @pl.when(kj < qi * ratio)
    def _():
        step(None)

    for d in range(ratio):
        @pl.when(kj == qi * ratio + d)
        def _(d=d):
            step(d)

    @pl.when(kj == (qi + 1) * ratio - 1)
    def _():
        for r in range(DSA_REP):
            o = acc_ref[r, :, :DSA_HEAD_DIM] / acc_ref[r, :, DSA_HEAD_DIM:]
            o_ref[:, r * DSA_HEAD_DIM:(r + 1) * DSA_HEAD_DIM] = o.astype(o_ref.dtype)


def _dsa_attn(qkv, bias, batch, *, tq=2048, tk=1024, rq=128, name):
    T = qkv.shape[0]
    S = T // batch
    tq, tk = min(tq, S), min(tk, S)
    rq = min(rq, tq)
    assert S % tq == 0 and tq % tk == 0 and tq % rq == 0
    nq, nk = S // tq, S // tk
    qi, kj = _causal_pairs(nq, tq // tk)
    G = DSA_KV_HEADS
    gw = DSA_REP * DSA_HEAD_DIM
    kb = DSA_Q // DSA_HEAD_DIM
    grid_spec = pltpu.PrefetchScalarGridSpec(
        num_scalar_prefetch=2,
        grid=(batch, G, int(qi.shape[0])),
        in_specs=[pl.BlockSpec((tq, gw), lambda b, g, p, qi, kj: (b * nq + qi[p], g)),
                  pl.BlockSpec((tk, DSA_HEAD_DIM), lambda b, g, p, qi, kj: (b * nk + kj[p], kb + g)),
                  pl.BlockSpec((tk, DSA_HEAD_DIM), lambda b, g, p, qi, kj: (b * nk + kj[p], kb + G + g)),
                  pl.BlockSpec((1, tq, tk), lambda b, g, p, qi, kj: (b, qi[p], kj[p]))],
        out_specs=pl.BlockSpec((tq, gw), lambda b, g, p, qi, kj: (b * nq + qi[p], g)),
        scratch_shapes=[pltpu.VMEM((DSA_REP, tq, DSA_HEAD_DIM), F32),
                        pltpu.VMEM((DSA_REP, tq, 2 * DSA_HEAD_DIM), F32)])
    return pl.pallas_call(
        functools.partial(_dsa_attn_kernel, tq=tq, tk=tk, rq=rq),
        grid_spec=grid_spec,
        out_shape=jax.ShapeDtypeStruct((T, DSA_Q), BF16),
        compiler_params=_cparams(("parallel", "parallel", "arbitrary")),
        name=name,
    )(qi, kj, qkv, qkv, qkv, bias)


def _even_mixer(x, positions, batch, w_in, w_out, lam_p, subln_g, conv_w, conv_b, conv_ln_g, conv_ln_b,
                lam_init, ln_g, ln_b):
    tabs, half = _rot_tables(positions, DIFF_HEAD_DIM)
    n_rot = (2 * DIFF_WIDTH) // LANES
    chunk_types = [0] * n_rot + [-1] * ((EVEN_IN - 2 * DIFF_WIDTH) // LANES)
    q_scale = DIFF_HEAD_DIM ** -0.5 * LOG2E
    chunk_scales = [q_scale] * (DIFF_WIDTH // LANES) + [1.0] * ((EVEN_IN - DIFF_WIDTH) // LANES)
    proj = _inproj(x, w_in.astype(BF16), tabs, (half,), chunk_types, BF16, tm=1024, tn=1024, name="even_inproj",
                   chunk_scales=chunk_scales)
    lp = lam_p.astype(F32)
    lam = jnp.exp(jnp.sum(lp[0] * lp[1])) - jnp.exp(jnp.sum(lp[2] * lp[3])) + lam_init
    a = _diff_attn(proj, lam, subln_g, batch, lam_init, name="diff_attn")
    c = _conv_module(proj, conv_w, conv_b, conv_ln_g, conv_ln_b, batch, name="conv_module")
    return _proj_ln((a, c), w_out.astype(BF16), x, ln_g, ln_b, name="even_outproj_ln")


def _odd_mixer(x, positions, batch, w_in, w_out, ln_g, ln_b):
    T = x.shape[0]
    tabs_b, half_b = _rot_tables(positions, DSA_HEAD_DIM)
    n_rot = (DSA_Q + DSA_KV) // LANES
    chunk_types = [0] * n_rot + [-1] * (DSA_KV // LANES)
    w_main = w_in[:, :ODD_MAIN].astype(BF16)
    q_scale = DSA_HEAD_DIM ** -0.5 * LOG2E
    chunk_scales = [q_scale] * (DSA_Q // LANES) + [1.0] * ((ODD_MAIN - DSA_Q) // LANES)
    qkv = _inproj(x, w_main, tabs_b, (half_b,), chunk_types, BF16, tm=1024, tn=1024, name="odd_inproj",
                  chunk_scales=chunk_scales)
    w_idx = jnp.pad(w_in[:, ODD_MAIN:], ((0, 0), (0, IDX_PAD - (w_in.shape[1] - ODD_MAIN)))).astype(BF16)
    tabs_a, half_a = _rot_tables(positions, IDX_DIM)
    tabs_ah, _ = _rot_tables(positions, IDX_DIM, active_lanes=IDX_DIM)
    idx_types = [0] * (IDX_Q // LANES) + [1] + [-1] * ((IDX_PAD - IDX_Q) // LANES - 1)
    idx = _inproj(x, w_idx, jnp.concatenate([tabs_a, tabs_ah]), (half_a, half_a), idx_types, F32,
                  tm=1024, tn=IDX_PAD, name="idx_inproj")
    kidx = idx[:, IDX_Q:IDX_Q + LANES].astype(BF16)
    bias = _indexer(idx, kidx, batch, name="indexer")
    o = _dsa_attn(qkv, bias, batch, name="dsa_attn")
    return _proj_ln((o,), w_out.astype(BF16), x, ln_g, ln_b, name="odd_outproj_ln")


def kernel(x, mem, positions, w_in_even, w_out_even, diff_lambda, diff_subln_g, conv_w, conv_b, conv_ln_g,
           conv_ln_b, w_in_odd, w_out_odd, xa_wq, xa_wkv, xa_wo, ffn_w_in, ffn_w_out, ln_g, ln_b):
    B, S, D = x.shape
    M = mem.shape[1]
    x = x.reshape(B * S, D)
    mem2 = mem.reshape(B * M, D)
    for layer in range(DEPTH):
        j = layer // 2
        if layer % 2 == 0:
            lam_init = 0.8 - 0.6 * math.exp(-0.3 * layer)
            x = _even_mixer(x, positions, B, w_in_even[j], w_out_even[j], diff_lambda[j], diff_subln_g[j],
                            conv_w[j], conv_b[j], conv_ln_g[j], conv_ln_b[j], lam_init,
                            ln_g[layer, 0], ln_b[layer, 0])
        else:
            x = _odd_mixer(x, positions, B, w_in_odd[j], w_out_odd[j], ln_g[layer, 0], ln_b[layer, 0])
        kv = _inproj(mem2, xa_wkv[layer].astype(BF16), None, (), [-1] * (2 * D // LANES), BF16,
                     tm=512, tn=512, name=f"xa_kvproj_{layer}")
        o = _xattn(x, xa_wq[layer].astype(BF16), kv, B, name=f"xattn_{layer}")
        x = _proj_ln((o,), xa_wo[layer].astype(BF16), x, ln_g[layer, 1], ln_b[layer, 1],
                     name=f"xa_outproj_ln_{layer}")
        x = _ffn(x, ffn_w_in[layer].astype(BF16), ffn_w_out[layer].astype(BF16),
                 ln_g[layer, 2], ln_b[layer, 2], name=f"ffn_{layer}")
    return x.reshape(B, S, D)
```

```python
import functools
import math

import jax
import jax.numpy as jnp
from jax import lax
from jax.experimental import pallas as pl
from jax.experimental.pallas import tpu as pltpu

F32 = jnp.float32
BF16 = jnp.bfloat16

D_MODEL = 2048
DEPTH = 2
ALPHA = (2 * DEPTH) ** 0.25
LN_EPS = 1e-5
ROPE_THETA = 500000.0
ROPE_FRAC = 4

DIFF_HEADS = 8
DIFF_HEAD_DIM = 64
DIFF_V_DIM = 2 * DIFF_HEAD_DIM
DIFF_WIDTH = DIFF_HEADS * DIFF_V_DIM
CONV_CH = D_MODEL - DIFF_WIDTH
CONV_WIDTH = 31
EVEN_IN = 3 * DIFF_WIDTH + 2 * CONV_CH

DSA_HEADS = 16
DSA_KV_HEADS = 4
DSA_HEAD_DIM = 128
DSA_REP = DSA_HEADS // DSA_KV_HEADS
IDX_HEADS = 4
IDX_DIM = 64
TOPK_MAX = 256
DSA_Q = DSA_HEADS * DSA_HEAD_DIM
DSA_KV = DSA_KV_HEADS * DSA_HEAD_DIM
IDX_Q = IDX_HEADS * IDX_DIM
ODD_MAIN = DSA_Q + 2 * DSA_KV
IDX_PAD = 512

XA_HEADS = 4
XA_HEAD_DIM = D_MODEL // XA_HEADS
D_FF = -(-(8 * D_MODEL) // (3 * 256)) * 256

LANES = 128
SUBLANES = 8
MXU_COLS = 256
NEG = -1e30
VMEM_LIMIT = 56 * 1024 * 1024


def _cparams(sem):
    return pltpu.CompilerParams(dimension_semantics=sem, vmem_limit_bytes=VMEM_LIMIT)


def _layer_norm_rows(y, g, b):
    mu = jnp.mean(y, axis=-1, keepdims=True)
    d = y - mu
    var = jnp.mean(d * d, axis=-1, keepdims=True)
    return d * lax.rsqrt(var + LN_EPS) * g + b


def _rot_tables(positions, head_dim, active_lanes=LANES):
    rot = head_dim // ROPE_FRAC
    half = rot // 2
    inv_freq = 1.0 / (ROPE_THETA ** (jnp.arange(half, dtype=F32) / half))
    ang = positions.reshape(-1, 1).astype(F32) * inv_freq
    cos, sin = jnp.cos(ang), jnp.sin(ang)
    lane = jnp.arange(LANES)
    d = lane % head_dim
    f = d % half
    live = lane < active_lanes
    cos_l = jnp.take(cos, f, axis=1)
    sin_l = jnp.take(sin, f, axis=1)
    c = jnp.where((d < rot) & live, cos_l, 1.0)
    s1 = jnp.where((d >= half) & (d < rot) & live, sin_l, 0.0)
    s2 = jnp.where((d < half) & live, -sin_l, 0.0)
    return jnp.stack([c, s1, s2]), half


def _inproj_kernel(x_ref, w_ref, tab_ref, o_ref, xb_ref, *, variants, shifts):
    j = pl.program_id(1)

    @pl.when(j == 0)
    def _():
        xb_ref[...] = x_ref[...].astype(BF16)

    acc = jnp.dot(xb_ref[...], w_ref[...], preferred_element_type=F32)

    for (j_lo, j_hi), types in variants:
        @pl.when((j >= j_lo) & (j < j_hi))
        def _(types=types):
            for c, (p, scale) in enumerate(types):
                blk = acc[:, c * LANES:(c + 1) * LANES]
                if p >= 0:
                    sh = shifts[p]
                    blk = (blk * tab_ref[3 * p]
                           + pltpu.roll(blk, sh, 1) * tab_ref[3 * p + 1]
                           + pltpu.roll(blk, LANES - sh, 1) * tab_ref[3 * p + 2])
                if scale != 1.0:
                    blk = blk * scale
                o_ref[:, c * LANES:(c + 1) * LANES] = blk.astype(o_ref.dtype)


def _inproj(x, w, tabs, shifts, chunk_types, out_dtype, *, tm, tn, name, chunk_scales=None):
    T, K = x.shape
    N = w.shape[1]
    tm = min(tm, T)
    assert T % tm == 0 and N % tn == 0 and len(chunk_types) == N // LANES
    if chunk_scales is None:
        chunk_scales = [1.0] * len(chunk_types)
    chunk_types = list(zip(chunk_types, chunk_scales))
    per = tn // LANES
    tile_types = [tuple(chunk_types[t * per:(t + 1) * per]) for t in range(N // tn)]
    variants = []
    for t, ty in enumerate(tile_types):
        if variants and variants[-1][1] == ty and variants[-1][0][1] == t:
            variants[-1] = ((variants[-1][0][0], t + 1), ty)
        else:
            variants.append(((t, t + 1), ty))
    if tabs is None:
        tabs = jnp.zeros((3, T, LANES), F32)
    P3 = tabs.shape[0]
    return pl.pallas_call(
        functools.partial(_inproj_kernel, variants=tuple(variants), shifts=tuple(shifts)),
        grid=(T // tm, N // tn),
        in_specs=[pl.BlockSpec((tm, K), lambda i, j: (i, 0)),
                  pl.BlockSpec((K, tn), lambda i, j: (0, j)),
                  pl.BlockSpec((P3, tm, LANES), lambda i, j: (0, i, 0))],
        out_specs=pl.BlockSpec((tm, tn), lambda i, j: (i, j)),
        out_shape=jax.ShapeDtypeStruct((T, N), out_dtype),
        scratch_shapes=[pltpu.VMEM((tm, K), BF16)],
        compiler_params=_cparams(("parallel", "arbitrary")),
        name=name,
    )(x, w, tabs)


def _proj_ln_kernel(*refs, widths):
    h_refs = refs[:len(widths)]
    w_ref, x_ref, g_ref, b_ref, o_ref = refs[len(widths):]
    y = ALPHA * x_ref[...]
    k0 = 0
    for h_ref, kw in zip(h_refs, widths):
        y = y + jnp.dot(h_ref[...], w_ref[k0:k0 + kw, :], preferred_element_type=F32)
        k0 += kw
    o_ref[...] = _layer_norm_rows(y, g_ref[...], b_ref[...])


def _proj_ln(hs, w, x, g, b, *, tm=512, name):
    T = x.shape[0]
    K, D = w.shape
    widths = tuple(h.shape[1] for h in hs)
    tm = min(tm, T)
    assert T % tm == 0 and sum(widths) == K
    return pl.pallas_call(
        functools.partial(_proj_ln_kernel, widths=widths),
        grid=(T // tm,),
        in_specs=[pl.BlockSpec((tm, kw), lambda i: (i, 0)) for kw in widths] + [
                  pl.BlockSpec((K, D), lambda i: (0, 0), pipeline_mode=pl.Buffered(1)),
                  pl.BlockSpec((tm, D), lambda i: (i, 0)),
                  pl.BlockSpec((1, D), lambda i: (0, 0)),
                  pl.BlockSpec((1, D), lambda i: (0, 0))],
        out_specs=pl.BlockSpec((tm, D), lambda i: (i, 0)),
        out_shape=jax.ShapeDtypeStruct((T, D), F32),
        compiler_params=_cparams(("parallel",)),
        name=name,
    )(*hs, w, x, g.reshape(1, D), b.reshape(1, D))


def _ffn_kernel(x_ref, wg_ref, wu_ref, wo_ref, g_ref, b_ref, o_ref, xb_ref, acc_ref):
    f = pl.program_id(1)

    @pl.when(f == 0)
    def _():
        xb_ref[...] = x_ref[...].astype(BF16)
        acc_ref[...] = jnp.zeros_like(acc_ref)

    xb = xb_ref[...]
    gate = jnp.dot(xb, wg_ref[...], preferred_element_type=F32)
    up = jnp.dot(xb, wu_ref[...], preferred_element_type=F32)
    h = (gate * jax.nn.sigmoid(gate) * up).astype(BF16)
    acc_ref[...] += jnp.dot(h, wo_ref[...], preferred_element_type=F32)

    @pl.when(f == pl.num_programs(1) - 1)
    def _():
        y = ALPHA * x_ref[...] + acc_ref[...]
        o_ref[...] = _layer_norm_rows(y, g_ref[...], b_ref[...])


def _ffn(x, w_in, w_out, g, b, *, tm=512, tf=512, name):
    T, D = x.shape
    dff = w_out.shape[0]
    tm = min(tm, T)
    assert T % tm == 0 and dff % tf == 0
    nf = dff // tf
    return pl.pallas_call(
        _ffn_kernel,
        grid=(T // tm, nf),
        in_specs=[pl.BlockSpec((tm, D), lambda i, f: (i, 0)),
                  pl.BlockSpec((D, tf), lambda i, f: (0, f)),
                  pl.BlockSpec((D, tf), lambda i, f: (0, f + nf)),
                  pl.BlockSpec((tf, D), lambda i, f: (f, 0)),
                  pl.BlockSpec((1, D), lambda i, f: (0, 0)),
                  pl.BlockSpec((1, D), lambda i, f: (0, 0))],
        out_specs=pl.BlockSpec((tm, D), lambda i, f: (i, 0)),
        out_shape=jax.ShapeDtypeStruct((T, D), F32),
        scratch_shapes=[pltpu.VMEM((tm, D), BF16), pltpu.VMEM((tm, D), F32)],
        compiler_params=_cparams(("parallel", "arbitrary")),
        name=name,
    )(x, w_in, w_in, w_out, g.reshape(1, D), b.reshape(1, D))


def _xattn_kernel(x_ref, wq_ref, k_ref, v_ref, o_ref):
    xb = x_ref[...].astype(BF16)
    q = jnp.dot(xb, wq_ref[...], preferred_element_type=F32).astype(BF16)
    scale = XA_HEAD_DIM ** -0.5
    for h in range(XA_HEADS):
        sl = slice(h * XA_HEAD_DIM, (h + 1) * XA_HEAD_DIM)
        s = lax.dot_general(q[:, sl], k_ref[:, sl], (((1,), (1,)), ((), ())),
                            preferred_element_type=F32) * scale
        m = jnp.max(s, axis=-1, keepdims=True)
        p = jnp.exp(s - m)
        l = jnp.sum(p, axis=-1, keepdims=True)
        p = (p / l).astype(BF16)
        o = jnp.dot(p, v_ref[:, sl], preferred_element_type=F32)
        o_ref[:, sl] = o.astype(o_ref.dtype)


def _xattn(x, wq, kv, batch, *, tm=512, name):
    T, D = x.shape
    S = T // batch
    M = kv.shape[0] // batch
    tm = min(tm, S)
    ns = S // tm
    return pl.pallas_call(
        _xattn_kernel,
        grid=(batch, ns),
        in_specs=[pl.BlockSpec((tm, D), lambda b, i: (b * ns + i, 0)),
                  pl.BlockSpec((D, D), lambda b, i: (0, 0)),
                  pl.BlockSpec((M, D), lambda b, i: (b, 0)),
                  pl.BlockSpec((M, D), lambda b, i: (b, 1))],
        out_specs=pl.BlockSpec((tm, D), lambda b, i: (b * ns + i, 0)),
        out_shape=jax.ShapeDtypeStruct((T, D), BF16),
        compiler_params=_cparams(("parallel", "arbitrary")),
        name=name,
    )(x, wq, kv, kv)


NT_DIMS = (((1,), (1,)), ((), ()))
LOG2E = math.log2(math.e)


def _causal_pairs(nq, ratio):
    qi = [i for i in range(nq) for _ in range((i + 1) * ratio)]
    kj = [j for i in range(nq) for j in range((i + 1) * ratio)]
    return jnp.asarray(qi, jnp.int32), jnp.asarray(kj, jnp.int32)


def _flash_update(q, k, v_ext, m_ref, acc_ref, idx, rows, bias=None, mask=None):
    s = lax.dot_general(q, k, NT_DIMS, preferred_element_type=F32)
    if bias is not None:
        s = s + bias
    if mask is not None:
        s = jnp.where(mask, s, NEG)
    chunks = [s[:, j * LANES:(j + 1) * LANES] for j in range(s.shape[1] // LANES)]
    mc = chunks[0]
    for ch in chunks[1:]:
        mc = jnp.maximum(mc, ch)
    m_prev = m_ref[idx, rows, :]
    m_new = jnp.maximum(m_prev, jnp.max(mc, axis=1, keepdims=True))
    p = jnp.concatenate([jnp.exp2((ch - m_new).astype(BF16)) for ch in chunks], axis=1)
    alpha = jnp.exp2(m_prev - m_new)
    pv = jnp.dot(p, v_ext, preferred_element_type=F32)
    acc_ref[idx, rows, :] = jnp.concatenate([alpha, alpha], axis=1) * acc_ref[idx, rows, :] + pv
    m_ref[idx, rows, :] = m_new


def _causal_width(rb, rq, d, tk):
    seen = (rb + 1) * rq - d * tk
    return min(tk, -(-seen // MXU_COLS) * MXU_COLS)


def _flash_init(m_ref, acc_ref):
    m_ref[...] = jnp.full_like(m_ref, -jnp.inf)
    acc_ref[...] = jnp.zeros_like(acc_ref)


def _with_ones(v):
    return jnp.concatenate([v, jnp.ones_like(v)], axis=1)


def _diff_attn_kernel(lam_ref, q_ref, k_ref, v_ref, g_ref, o_ref, m_ref, acc_ref, *, tq, tk, rq, lam_init):
    qi = pl.program_id(2)
    ratio = tq // tk
    _flash_init(m_ref, acc_ref)

    def step(kj, d):
        k0 = pl.multiple_of(kj * tk, tk)
        k = k_ref[pl.ds(k0, tk), :]
        v_ext = _with_ones(v_ref[pl.ds(k0, tk), :])
        lane = lax.broadcasted_iota(jnp.int32, (rq, LANES), 1)
        for rb in range(tq // rq):
            if d is not None and (rb + 1) * rq <= d * tk:
                continue
            rows = slice(rb * rq, (rb + 1) * rq)
            q = q_ref[rows, :]
            mask = None
            nk = tk if d is None else _causal_width(rb, rq, d, tk)
            if d is not None and rb * rq < d * tk + nk - 1:
                row = rb * rq + lax.broadcasted_iota(jnp.int32, (rq, nk), 0)
                col = d * tk + lax.broadcasted_iota(jnp.int32, (rq, nk), 1)
                mask = col <= row
            for c, sel in enumerate((lane < DIFF_HEAD_DIM, lane >= DIFF_HEAD_DIM)):
                _flash_update(jnp.where(sel, q, jnp.zeros_like(q)), k[:nk], v_ext[:nk], m_ref, acc_ref, c, rows,
                              mask=mask)

    def below_diagonal(kj, carry):
        step(kj, None)
        return carry

    lax.fori_loop(0, qi * ratio, below_diagonal, 0)
    for d in range(ratio):
        step(qi * ratio + d, d)

    lam = lam_ref[0]
    a = (acc_ref[0, :, :LANES] / acc_ref[0, :, LANES:]
         - lam * (acc_ref[1, :, :LANES] / acc_ref[1, :, LANES:]))
    ms = jnp.mean(a * a, axis=-1, keepdims=True)
    a = a * lax.rsqrt(ms + LN_EPS) * g_ref[...]
    o_ref[...] = (a * (1.0 - lam_init)).astype(o_ref.dtype)


def _diff_attn(proj, lam, subln_g, batch, lam_init, *, tq=2048, tk=1024, rq=128, name):
    T = proj.shape[0]
    S = T // batch
    tq, tk = min(tq, S), min(tk, S)
    rq = min(rq, tq)
    assert S % tq == 0 and tq % tk == 0 and tq % rq == 0
    nq = S // tq
    H = DIFF_HEADS
    return pl.pallas_call(
        functools.partial(_diff_attn_kernel, tq=tq, tk=tk, rq=rq, lam_init=lam_init),
        grid=(batch, H, nq),
        in_specs=[pl.BlockSpec(memory_space=pltpu.SMEM),
                  pl.BlockSpec((tq, LANES), lambda b, h, i: (b * nq + i, h)),
                  pl.BlockSpec((S, LANES), lambda b, h, i: (b, H + h)),
                  pl.BlockSpec((S, LANES), lambda b, h, i: (b, 2 * H + h)),
                  pl.BlockSpec((1, LANES), lambda b, h, i: (0, 0))],
        out_specs=pl.BlockSpec((tq, LANES), lambda b, h, i: (b * nq + i, h)),
        out_shape=jax.ShapeDtypeStruct((T, DIFF_WIDTH), BF16),
        scratch_shapes=[pltpu.VMEM((2, tq, LANES), F32), pltpu.VMEM((2, tq, 2 * LANES), F32)],
        compiler_params=_cparams(("parallel", "parallel", "arbitrary")),
        name=name,
    )(lam.reshape(1).astype(F32), proj, proj, proj, subln_g.reshape(1, LANES).astype(F32))


CONV_HALO = 32
CONV_ROWS = 64


def _conv_kernel(val_ref, gate_ref, hval_ref, hgate_ref, cw_ref, cb_ref, g_ref, b_ref, o_ref,
                 u_ref, c_ref, *, tr):
    i = pl.program_id(1)
    rows = CONV_HALO + tr
    n_ch = CONV_CH // LANES
    halo_on = jnp.where(i > 0, 1.0, 0.0)
    for c in range(n_ch):
        cs = slice(c * LANES, (c + 1) * LANES)
        u_ref[0, c, CONV_HALO:rows, :] = val_ref[:, cs].astype(F32) * jax.nn.sigmoid(gate_ref[:, cs].astype(F32))
        halo = hval_ref[:, cs].astype(F32) * jax.nn.sigmoid(hgate_ref[:, cs].astype(F32))
        u_ref[0, c, :CONV_HALO, :] = halo * halo_on
        for e in range(1, SUBLANES):
            u_ref[e, c, :rows - SUBLANES, :] = u_ref[0, c, e:rows - SUBLANES + e, :]
    base = CONV_HALO - (CONV_WIDTH - 1)
    n_rb = tr // CONV_ROWS
    for c in range(n_ch):
        cs = slice(c * LANES, (c + 1) * LANES)
        accs = [jnp.broadcast_to(cb_ref[:, cs], (CONV_ROWS, LANES))] * n_rb
        for w in range(CONV_WIDTH):
            e = (base + w) % SUBLANES
            tap = jnp.broadcast_to(cw_ref[w:w + 1, cs], (CONV_ROWS, LANES))
            for r in range(n_rb):
                r0 = r * CONV_ROWS + base + w - e
                accs[r] = accs[r] + u_ref[e, c, r0:r0 + CONV_ROWS, :] * tap
        for r in range(n_rb):
            c_ref[c, r * CONV_ROWS:(r + 1) * CONV_ROWS, :] = accs[r]
    conv = jnp.concatenate([c_ref[c] for c in range(n_ch)], axis=1)
    y = _layer_norm_rows(conv, g_ref[...], b_ref[...])
    o_ref[...] = (y * jax.nn.sigmoid(y)).astype(o_ref.dtype)


def _conv_module(proj, conv_w, conv_b, ln_g, ln_b, batch, *, tr=256, name):
    T = proj.shape[0]
    S = T // batch
    tr = min(tr, S)
    ns = S // tr
    vb = (3 * DIFF_WIDTH) // CONV_CH
    hpb = tr // CONV_HALO

    def halo_idx(col):
        return lambda b, i: (jnp.maximum((b * ns + i) * hpb - 1, 0), col)

    vec = lambda a: a.reshape(1, CONV_CH).astype(F32)
    return pl.pallas_call(
        functools.partial(_conv_kernel, tr=tr),
        grid=(batch, ns),
        in_specs=[pl.BlockSpec((tr, CONV_CH), lambda b, i: (b * ns + i, vb)),
                  pl.BlockSpec((tr, CONV_CH), lambda b, i: (b * ns + i, vb + 1)),
                  pl.BlockSpec((CONV_HALO, CONV_CH), halo_idx(vb)),
                  pl.BlockSpec((CONV_HALO, CONV_CH), halo_idx(vb + 1)),
                  pl.BlockSpec((CONV_WIDTH, CONV_CH), lambda b, i: (0, 0)),
                  pl.BlockSpec((1, CONV_CH), lambda b, i: (0, 0)),
                  pl.BlockSpec((1, CONV_CH), lambda b, i: (0, 0)),
                  pl.BlockSpec((1, CONV_CH), lambda b, i: (0, 0))],
        out_specs=pl.BlockSpec((tr, CONV_CH), lambda b, i: (b * ns + i, 0)),
        out_shape=jax.ShapeDtypeStruct((T, CONV_CH), BF16),
        scratch_shapes=[pltpu.VMEM((SUBLANES, CONV_CH // LANES, CONV_HALO + tr, LANES), F32),
                        pltpu.VMEM((CONV_CH // LANES, tr, LANES), F32)],
        compiler_params=_cparams(("parallel", "arbitrary")),
        name=name,
    )(proj, proj, proj, proj, conv_w.astype(F32), vec(conv_b), vec(ln_g), vec(ln_b))


IDX_CHUNK = 1024
SLABS = IDX_CHUNK // LANES
KEY16_LO = -32640
KEY16_HI = 0x7F80
KEY16_BITS = 16
GALLOP_START = 256


def _key_to_f32(key):
    bits = jnp.where(key < 0, key ^ jnp.int32(0x7FFFFFFF), key)
    return pltpu.bitcast(bits, F32)


def _indexer_kernel(q_ref, k_ref, o_ref, sc_ref, g_ref, *, tq, topk, seq):
    i = pl.program_id(1)
    q0 = i * tq
    nch = (q0 + tq + IDX_CHUNK - 1) // IDX_CHUNK
    n_all = seq // IDX_CHUNK
    lane = lax.broadcasted_iota(jnp.int32, (tq, LANES), 1)

    qh = []
    for blk in range(IDX_Q // LANES):
        qb = q_ref[:, blk * LANES:(blk + 1) * LANES]
        qh.append(jnp.where(lane < IDX_DIM, qb, 0.0).astype(BF16))
        qh.append(jnp.where(lane < IDX_DIM, pltpu.roll(qb, IDX_DIM, 1), 0.0).astype(BF16))
    wcol = IDX_Q + IDX_DIM
    wscale = IDX_HEADS ** -0.5 * IDX_DIM ** -0.5
    wts = [q_ref[:, wcol + h:wcol + h + 1] * wscale for h in range(IDX_HEADS)]

    def score_chunk(c, masked):
        k0 = pl.multiple_of(c * IDX_CHUNK, IDX_CHUNK)
        kc = k_ref[pl.ds(k0, IDX_CHUNK), :]
        sc = None
        for h in range(IDX_HEADS):
            r = lax.dot_general(qh[h], kc, (((1,), (1,)), ((), ())), preferred_element_type=F32)
            t = wts[h] * jnp.maximum(r, 0.0)
            sc = t if sc is None else sc + t
        if masked:
            row = q0 + lax.broadcasted_iota(jnp.int32, (tq, IDX_CHUNK), 0)
            col = k0 + lax.broadcasted_iota(jnp.int32, (tq, IDX_CHUNK), 1)
            sc = jnp.where(col <= row, sc, -jnp.inf)
        hi_bits = pltpu.bitcast(sc, jnp.int32) & jnp.int32(-(1 << KEY16_BITS))
        g = pltpu.bitcast(hi_bits, F32).astype(BF16)
        for j in range(SLABS):
            sc_ref[c * SLABS + j] = sc[:, j * LANES:(j + 1) * LANES]
            g_ref[c * SLABS + j] = g[:, j * LANES:(j + 1) * LANES]

    def load_scores(c):
        return jnp.concatenate([sc_ref[c * SLABS + j] for j in range(SLABS)], axis=1)

    def unmasked_body(c, carry):
        score_chunk(c, False)
        return carry

    lax.fori_loop(0, nch - 1, unmasked_body, 0)
    score_chunk(nch - 1, True)

    def count_in(ref, dtype, pred):
        one = jnp.ones((tq, LANES), dtype)
        zero = jnp.zeros((tq, LANES), dtype)

        def body(c, acc):
            for j in range(SLABS):
                acc = acc + jnp.where(pred(ref[c * SLABS + j]), one, zero)
            return acc
        acc = lax.fori_loop(0, nch, body, zero)
        return jnp.broadcast_to(jnp.sum(acc.astype(F32), axis=1, keepdims=True), (tq, LANES))

    def count(pred):
        return count_in(sc_ref, F32, pred)

    kf = float(topk)
    def max_body(c, acc):
        for j in range(SLABS):
            acc = jnp.maximum(acc, g_ref[c * SLABS + j])
        return acc
    gmax = lax.fori_loop(0, nch, max_body, jnp.full((tq, LANES), -jnp.inf, BF16))
    gmax = jnp.broadcast_to(jnp.max(gmax.astype(F32), axis=1, keepdims=True), (tq, LANES))
    mbits = pltpu.bitcast(gmax, jnp.int32)
    mkey = jnp.where(mbits < 0, mbits ^ jnp.int32(0x7FFFFFFF), mbits)
    hi16 = jnp.minimum((mkey >> KEY16_BITS) + 1, KEY16_HI)

    def searching16(carry):
        lo, hi, _, gallop, _ = carry
        return jnp.max(jnp.where(gallop > 0, 2, hi - lo)) > 1

    def search16(carry):
        lo, hi, step, gallop, cnt_lo = carry
        probe = jnp.where(gallop > 0, jnp.maximum(hi - step, KEY16_LO), (lo + hi) >> 1)
        key = jnp.where(probe < 0, (probe << KEY16_BITS) | jnp.int32(0xFFFF), probe << KEY16_BITS)
        cand = _key_to_f32(key).astype(BF16)
        cnt = count_in(g_ref, BF16, lambda blk: blk >= cand)
        ok = cnt >= kf
        at_floor = jnp.where(ok, 0, jnp.where(probe == KEY16_LO, gallop, 0))
        lo = jnp.where(ok, probe, lo)
        hi = jnp.where(ok, hi, jnp.where(at_floor > 0, KEY16_LO + 1, probe))
        step = jnp.where(ok, step, jnp.minimum(step * 2, 1 << KEY16_BITS))
        gallop = jnp.where(ok, 0, jnp.where(at_floor > 0, 0, gallop))
        return lo, hi, step, gallop, jnp.where(ok, cnt, cnt_lo)

    lo16, _, _, _, cnt_lo = lax.while_loop(
        searching16, search16,
        (jnp.full((tq, LANES), KEY16_LO, jnp.int32), hi16, jnp.full((tq, LANES), GALLOP_START, jnp.int32),
         jnp.ones((tq, LANES), jnp.int32), jnp.zeros((tq, LANES), F32)))

    def unresolved(carry):
        lo, hi, _ = carry
        return jnp.max(hi - lo) > 1

    def bisect32(carry):
        lo, hi, cnt_lo = carry
        mid = lo + ((hi - lo) >> 1)
        cand = _key_to_f32(mid)
        cnt = count(lambda blk: blk >= cand)
        ok = cnt >= kf
        hi = jnp.where(cnt == kf, mid + 1, jnp.where(ok, hi, mid))
        return jnp.where(ok, mid, lo), hi, jnp.where(ok, cnt, cnt_lo)

    lo32 = lo16 << KEY16_BITS
    at_start = count(lambda blk: blk > _key_to_f32(lo32)) < kf
    hi32 = jnp.where(at_start, lo32 + 1, (lo16 + 1) << KEY16_BITS)
    lo, _, n_ge = lax.while_loop(unresolved, bisect32, (lo32, hi32, cnt_lo))
    thr = _key_to_f32(lo)
    has_tie = jnp.max(n_ge) > kf
    thr_c = thr[:, :1]

    @pl.when(jnp.logical_not(has_tie))
    def _():
        def body(c, carry):
            k0 = pl.multiple_of(c * IDX_CHUNK, IDX_CHUNK)
            sc = load_scores(c)
            o_ref[0, :, pl.ds(k0, IDX_CHUNK)] = jnp.where(sc >= thr_c, 0.0, NEG).astype(o_ref.dtype)
            return carry
        lax.fori_loop(0, nch, body, 0)

    @pl.when(has_tie)
    def _():
        r_i = lax.broadcasted_iota(jnp.int32, (IDX_CHUNK, IDX_CHUNK), 0)
        c_i = lax.broadcasted_iota(jnp.int32, (IDX_CHUNK, IDX_CHUNK), 1)
        upper = jnp.where(r_i < c_i, 1.0, 0.0).astype(BF16)
        need_c = (kf - count(lambda blk: blk > thr))[:, :1]
        tie_c = jnp.where(n_ge > kf, 1.0, 0.0)[:, :1]

        def pending(carry):
            c, seen = carry
            return (c < nch) & (jnp.max(tie_c * (need_c - seen)) > 0.0)

        def body(carry):
            c, seen = carry
            k0 = pl.multiple_of(c * IDX_CHUNK, IDX_CHUNK)
            sc = load_scores(c)
            eq = jnp.where(sc == thr_c, 1.0, 0.0)
            before = seen + jnp.dot(eq.astype(BF16), upper, preferred_element_type=F32)
            keep = (sc > thr_c) | ((sc == thr_c) & (before < need_c))
            o_ref[0, :, pl.ds(k0, IDX_CHUNK)] = jnp.where(keep, 0.0, NEG).astype(o_ref.dtype)
            return c + 1, seen + jnp.sum(eq, axis=1, keepdims=True)
        c_done, _ = lax.while_loop(pending, body, (jnp.int32(0), jnp.zeros((tq, 1), F32)))

        def rest(c, carry):
            k0 = pl.multiple_of(c * IDX_CHUNK, IDX_CHUNK)
            sc = load_scores(c)
            keep = (sc > thr_c) | ((sc == thr_c) & (tie_c < 0.5))
            o_ref[0, :, pl.ds(k0, IDX_CHUNK)] = jnp.where(keep, 0.0, NEG).astype(o_ref.dtype)
            return carry
        lax.fori_loop(c_done, nch, rest, 0)

    def fill(c, carry):
        k0 = pl.multiple_of(c * IDX_CHUNK, IDX_CHUNK)
        o_ref[0, :, pl.ds(k0, IDX_CHUNK)] = jnp.full((tq, IDX_CHUNK), NEG, o_ref.dtype)
        return carry
    lax.fori_loop(nch, n_all, fill, 0)


def _indexer(idx, kidx, batch, *, tq=128, name):
    T = idx.shape[0]
    S = T // batch
    tq = min(tq, S)
    ns = S // tq
    topk = min(TOPK_MAX, S // 4)
    assert S % IDX_CHUNK == 0 and IDX_CHUNK % tq == 0
    return pl.pallas_call(
        functools.partial(_indexer_kernel, tq=tq, topk=topk, seq=S),
        grid=(batch, ns),
        in_specs=[pl.BlockSpec((tq, IDX_PAD), lambda b, i: (b * ns + i, 0)),
                  pl.BlockSpec((S, LANES), lambda b, i: (b, 0))],
        out_specs=pl.BlockSpec((1, tq, S), lambda b, i: (b, i, 0)),
        out_shape=jax.ShapeDtypeStruct((batch, S, S), BF16),
        scratch_shapes=[pltpu.VMEM((S // LANES, tq, LANES), F32), pltpu.VMEM((S // LANES, tq, LANES), BF16)],
        compiler_params=_cparams(("parallel", "arbitrary")),
        name=name,
    )(idx, kidx)


def _dsa_attn_kernel(qi_ref, kj_ref, q_ref, k_ref, v_ref, bias_ref, o_ref, m_ref, acc_ref, *, tq, tk, rq):
    p_id = pl.program_id(2)
    qi = qi_ref[p_id]
    kj = kj_ref[p_id]
    ratio = tq // tk

    @pl.when(kj == 0)
    def _():
        _flash_init(m_ref, acc_ref)

    def step(d):
        k = k_ref[...]
        v_ext = _with_ones(v_ref[...])
        for rb in range(tq // rq):
            if d is not None and (rb + 1) * rq <= d * tk:
                continue
            rows = slice(rb * rq, (rb + 1) * rq)
            nk = tk if d is None else _causal_width(rb, rq, d, tk)
            bias = bias_ref[0, rows, :nk].astype(F32)
            for r in range(DSA_REP):
                q = q_ref[rows, r * DSA_HEAD_DIM:(r + 1) * DSA_HEAD_DIM]
                _flash_update(q, k[:nk], v_ext[:nk], m_ref, acc_ref, r, rows, bias=bias)

    @pl.when(kj < qi * ratio)
    def _():
        step(None)

    for d in range(ratio):
        @pl.when(kj == qi * ratio + d)
        def _(d=d):
            step(d)

    @pl.when(kj == (qi + 1) * ratio - 1)
    def _():
        for r in range(DSA_REP):
            o = acc_ref[r, :, :DSA_HEAD_DIM] / acc_ref[r, :, DSA_HEAD_DIM:]
            o_ref[:, r * DSA_HEAD_DIM:(r + 1) * DSA_HEAD_DIM] = o.astype(o_ref.dtype)


def _dsa_attn(qkv, bias, batch, *, tq=2048, tk=1024, rq=128, name):
    T = qkv.shape[0]
    S = T // batch
    tq, tk = min(tq, S), min(tk, S)
    rq = min(rq, tq)
    assert S % tq == 0 and tq % tk == 0 and tq % rq == 0
    nq, nk = S // tq, S // tk
    qi, kj = _causal_pairs(nq, tq // tk)
    G = DSA_KV_HEADS
    gw = DSA_REP * DSA_HEAD_DIM
    kb = DSA_Q // DSA_HEAD_DIM
    grid_spec = pltpu.PrefetchScalarGridSpec(
        num_scalar_prefetch=2,
        grid=(batch, G, int(qi.shape[0])),
        in_specs=[pl.BlockSpec((tq, gw), lambda b, g, p, qi, kj: (b * nq + qi[p], g)),
                  pl.BlockSpec((tk, DSA_HEAD_DIM), lambda b, g, p, qi, kj: (b * nk + kj[p], kb + g)),
                  pl.BlockSpec((tk, DSA_HEAD_DIM), lambda b, g, p, qi, kj: (b * nk + kj[p], kb + G + g)),
                  pl.BlockSpec((1, tq, tk), lambda b, g, p, qi, kj: (b, qi[p], kj[p]))],
        out_specs=pl.BlockSpec((tq, gw), lambda b, g, p, qi, kj: (b * nq + qi[p], g)),
        scratch_shapes=[pltpu.VMEM((DSA_REP, tq, DSA_HEAD_DIM), F32),
                        pltpu.VMEM((DSA_REP, tq, 2 * DSA_HEAD_DIM), F32)])
    return pl.pallas_call(
        functools.partial(_dsa_attn_kernel, tq=tq, tk=tk, rq=rq),
        grid_spec=grid_spec,
        out_shape=jax.ShapeDtypeStruct((T, DSA_Q), BF16),
        compiler_params=_cparams(("parallel", "parallel", "arbitrary")),
        name=name,
    )(qi, kj, qkv, qkv, qkv, bias)


def _even_mixer(x, positions, batch, w_in, w_out, lam_p, subln_g, conv_w, conv_b, conv_ln_g, conv_ln_b,
                lam_init, ln_g, ln_b):
    tabs, half = _rot_tables(positions, DIFF_HEAD_DIM)
    n_rot = (2 * DIFF_WIDTH) // LANES
    chunk_types = [0] * n_rot + [-1] * ((EVEN_IN - 2 * DIFF_WIDTH) // LANES)
    q_scale = DIFF_HEAD_DIM ** -0.5 * LOG2E
    chunk_scales = [q_scale] * (DIFF_WIDTH // LANES) + [1.0] * ((EVEN_IN - DIFF_WIDTH) // LANES)
    proj = _inproj(x, w_in.astype(BF16), tabs, (half,), chunk_types, BF16, tm=1024, tn=1024, name="even_inproj",
                   chunk_scales=chunk_scales)
    lp = lam_p.astype(F32)
    lam = jnp.exp(jnp.sum(lp[0] * lp[1])) - jnp.exp(jnp.sum(lp[2] * lp[3])) + lam_init
    a = _diff_attn(proj, lam, subln_g, batch, lam_init, name="diff_attn")
    c = _conv_module(proj, conv_w, conv_b, conv_ln_g, conv_ln_b, batch, name="conv_module")
    return _proj_ln((a, c), w_out.astype(BF16), x, ln_g, ln_b, name="even_outproj_ln")


def _odd_mixer(x, positions, batch, w_in, w_out, ln_g, ln_b):
    T = x.shape[0]
    tabs_b, half_b = _rot_tables(positions, DSA_HEAD_DIM)
    n_rot = (DSA_Q + DSA_KV) // LANES
    chunk_types = [0] * n_rot + [-1] * (DSA_KV // LANES)
    w_main = w_in[:, :ODD_MAIN].astype(BF16)
    q_scale = DSA_HEAD_DIM ** -0.5 * LOG2E
    chunk_scales = [q_scale] * (DSA_Q // LANES) + [1.0] * ((ODD_MAIN - DSA_Q) // LANES)
    qkv = _inproj(x, w_main, tabs_b, (half_b,), chunk_types, BF16, tm=1024, tn=1024, name="odd_inproj",
                  chunk_scales=chunk_scales)
    w_idx = jnp.pad(w_in[:, ODD_MAIN:], ((0, 0), (0, IDX_PAD - (w_in.shape[1] - ODD_MAIN)))).astype(BF16)
    tabs_a, half_a = _rot_tables(positions, IDX_DIM)
    tabs_ah, _ = _rot_tables(positions, IDX_DIM, active_lanes=IDX_DIM)
    idx_types = [0] * (IDX_Q // LANES) + [1] + [-1] * ((IDX_PAD - IDX_Q) // LANES - 1)
    idx = _inproj(x, w_idx, jnp.concatenate([tabs_a, tabs_ah]), (half_a, half_a), idx_types, F32,
                  tm=1024, tn=IDX_PAD, name="idx_inproj")
    kidx = idx[:, IDX_Q:IDX_Q + LANES].astype(BF16)
    bias = _indexer(idx, kidx, batch, name="indexer")
    o = _dsa_attn(qkv, bias, batch, name="dsa_attn")
    return _proj_ln((o,), w_out.astype(BF16), x, ln_g, ln_b, name="odd_outproj_ln")


def kernel(x, mem, positions, w_in_even, w_out_even, diff_lambda, diff_subln_g, conv_w, conv_b, conv_ln_g,
           conv_ln_b, w_in_odd, w_out_odd, xa_wq, xa_wkv, xa_wo, ffn_w_in, ffn_w_out, ln_g, ln_b):
    B, S, D = x.shape
    M = mem.shape[1]
    x = x.reshape(B * S, D)
    mem2 = mem.reshape(B * M, D)
    for layer in range(DEPTH):
        j = layer // 2
        if layer % 2 == 0:
            lam_init = 0.8 - 0.6 * math.exp(-0.3 * layer)
            x = _even_mixer(x, positions, B, w_in_even[j], w_out_even[j], diff_lambda[j], diff_subln_g[j],
                            conv_w[j], conv_b[j], conv_ln_g[j], conv_ln_b[j], lam_init,
                            ln_g[layer, 0], ln_b[layer, 0])
        else:
            x = _odd_mixer(x, positions, B, w_in_odd[j], w_out_odd[j], ln_g[layer, 0], ln_b[layer, 0])
        kv = _inproj(mem2, xa_wkv[layer].astype(BF16), None, (), [-1] * (2 * D // LANES), BF16,
                     tm=512, tn=512, name=f"xa_kvproj_{layer}")
        o = _xattn(x, xa_wq[layer].astype(BF16), kv, B, name=f"xattn_{layer}")
        x = _proj_ln((o,), xa_wo[layer].astype(BF16), x, ln_g[layer, 1], ln_b[layer, 1],
                     name=f"xa_outproj_ln_{layer}")
        x = _ffn(x, ffn_w_in[layer].astype(BF16), ffn_w_out[layer].astype(BF16),
                 ln_g[layer, 2], ln_b[layer, 2], name=f"ffn_{layer}")
    return x.reshape(B, S, D)
```
